```python
import jax
import jax.numpy as jnp
from jax import lax
import numpy as np


D_MODEL = 1024
BATCH = 4
SEQ = 8192
DEPTH = 2

GRID_W = 64
CTX_LEN = 256
EPS = 1e-6
NEG_INF = -1e30

HEAD_DIM = 64
A_Q_HEADS = 8
A_KV_HEADS = 2
A_GROUP = A_Q_HEADS // A_KV_HEADS
B_Q_HEADS = 8
B_KV_HEADS = 2
B_GROUP = B_Q_HEADS // B_KV_HEADS
WINDOW = 128
Q_BLOCK = 128
BAND_SPAN = Q_BLOCK + 2 * WINDOW
ROPE_THETA = 10000.0
ROPE_PAIRS = HEAD_DIM // 4
A_QD = A_Q_HEADS * HEAD_DIM
A_KVD = A_KV_HEADS * HEAD_DIM
B_QD = B_Q_HEADS * HEAD_DIM
B_KVD = B_KV_HEADS * HEAD_DIM
ATTN_SPLITS = (A_QD, A_QD + A_KVD, A_QD + 2 * A_KVD, A_QD + 2 * A_KVD + B_QD, A_QD + 2 * A_KVD + B_QD + B_KVD)
ATTN_IN_DIM = A_QD + 2 * A_KVD + B_QD + 2 * B_KVD
ATTN_OUT_DIM = A_QD + B_QD

GLA_HEADS = 4
GLA_DK = D_MODEL // (2 * GLA_HEADS)
GLA_DV = D_MODEL // GLA_HEADS
GLA_KD = GLA_HEADS * GLA_DK
GLA_VD = GLA_HEADS * GLA_DV
GLA_RANK = 16
GLA_GATE_NORM = 16.0
GLA_CHUNK = 64
GLA_SPLITS = (GLA_KD, 2 * GLA_KD, 2 * GLA_KD + GLA_VD, 2 * GLA_KD + 2 * GLA_VD, 2 * GLA_KD + 2 * GLA_VD + GLA_RANK)
GLA_IN_DIM = 2 * GLA_KD + 2 * GLA_VD + 2 * GLA_RANK

PEER_HEADS = 8
PEER_N_KEYS = 128
PEER_N_EXPERTS = PEER_N_KEYS * PEER_N_KEYS
PEER_QUERY_DIM = 256
PEER_HALF = PEER_QUERY_DIM // 2
PEER_TOPK = 16
PEER_TOKEN_BLOCK = 128

N_EVEN = (DEPTH + 1) // 2
N_ODD = DEPTH // 2

kernel_name = 'hybrid_latent_dit_attn_gla_peer'


def rmsnorm(x, gain):
    x32 = x.astype(jnp.float32)
    y = x32 * lax.rsqrt(jnp.mean(x32 * x32, axis=-1, keepdims=True) + EPS)
    return (y * gain.astype(jnp.float32)).astype(x.dtype)


def adaln(cond, w, b):
    mod = jax.nn.silu(cond) @ w + b
    return jnp.split(mod[..., None, :], 6, axis=-1)


def axial_rope_tables(rows):
    row = jnp.repeat(jnp.arange(rows), GRID_W)
    col = jnp.tile(jnp.arange(GRID_W), rows)
    pos = jnp.stack([row, col], axis=-1).astype(jnp.float32)
    inv = ROPE_THETA ** (-jnp.arange(ROPE_PAIRS, dtype=jnp.float32) / ROPE_PAIRS)
    ang = pos[:, :, None] * inv
    return jnp.cos(ang), jnp.sin(ang)


def apply_rope(x, cos, sin):
    shape = x.shape
    xs = x.astype(jnp.float32).reshape(shape[:-1] + (2, 2, ROPE_PAIRS))
    bshape = (1, shape[1]) + (1,) * (len(shape) - 3) + (2, ROPE_PAIRS)
    cs, sn = cos.reshape(bshape), sin.reshape(bshape)
    x1, x2 = xs[..., 0, :], xs[..., 1, :]
    out = jnp.stack([x1 * cs - x2 * sn, x1 * sn + x2 * cs], axis=-2)
    return out.reshape(shape).astype(x.dtype)


def softmax_with_sink(s, sink):
    col = jnp.broadcast_to(sink.astype(jnp.float32)[None, :, :, None, None], s.shape[:-1] + (1,))
    return jax.nn.softmax(jnp.concatenate([s, col], axis=-1), axis=-1)[..., :-1]


def split_attn_projection(p):
    bsz, length, _ = p.shape
    qa, ka, va, qb, kb, vb = jnp.split(p, ATTN_SPLITS, axis=-1)
    return (qa.reshape(bsz, length, A_KV_HEADS, A_GROUP, HEAD_DIM),
            ka.reshape(bsz, length, A_KV_HEADS, HEAD_DIM),
            va.reshape(bsz, length, A_KV_HEADS, HEAD_DIM),
            qb.reshape(bsz, length, B_KV_HEADS, B_GROUP, HEAD_DIM),
            kb.reshape(bsz, length, B_KV_HEADS, HEAD_DIM),
            vb.reshape(bsz, length, B_KV_HEADS, HEAD_DIM))


def window_attention_latent(q, k, v, kc, vc, sink):
    bsz, length = q.shape[:2]
    n_blocks = length // Q_BLOCK
    pad = ((0, 0), (WINDOW, WINDOW), (0, 0), (0, 0))
    kp, vp = jnp.pad(k, pad), jnp.pad(v, pad)
    offs = jnp.arange(BAND_SPAN) - WINDOW
    band = jnp.abs(jnp.arange(Q_BLOCK)[:, None] - offs[None, :]) <= WINDOW
    ctx_ok = jnp.ones((Q_BLOCK, kc.shape[1]), dtype=bool)

    def block(n):
        start = n * Q_BLOCK
        qb = lax.dynamic_slice_in_dim(q, start, Q_BLOCK, axis=1)
        kb = jnp.concatenate([lax.dynamic_slice_in_dim(kp, start, BAND_SPAN, axis=1), kc], axis=1)
        vb = jnp.concatenate([lax.dynamic_slice_in_dim(vp, start, BAND_SPAN, axis=1), vc], axis=1)
        pos = start + offs
        valid = jnp.concatenate([band & ((pos >= 0) & (pos < length))[None, :], ctx_ok], axis=1)
        s = jnp.einsum('bqhgd,bkhd->bhgqk', qb, kb).astype(jnp.float32)
        p = softmax_with_sink(jnp.where(valid, s, NEG_INF), sink)
        return jnp.einsum('bhgqk,bkhd->bqhgd', p.astype(vb.dtype), vb)

    out = lax.map(block, jnp.arange(n_blocks))
    return jnp.moveaxis(out, 0, 1).reshape(bsz, length, -1)


def dense_attention_latent(q, k_all, v_all):
    bsz, length = q.shape[:2]

    def block(n):
        qb = lax.dynamic_slice_in_dim(q, n * Q_BLOCK, Q_BLOCK, axis=1)
        s = jnp.einsum('bqhgd,bkhd->bhgqk', qb, k_all).astype(jnp.float32)
        p = jax.nn.softmax(s, axis=-1)
        return jnp.einsum('bhgqk,bkhd->bqhgd', p.astype(v_all.dtype), v_all)

    out = lax.map(block, jnp.arange(length // Q_BLOCK))
    return jnp.moveaxis(out, 0, 1).reshape(bsz, length, -1)


def context_attention(q, k, v, sink):
    s = jnp.einsum('bqhgd,bkhd->bhgqk', q, k).astype(jnp.float32)
    p = jax.nn.softmax(s, axis=-1) if sink is None else softmax_with_sink(s, sink)
    o = jnp.einsum('bhgqk,bkhd->bqhgd', p.astype(v.dtype), v)
    return o.reshape(o.shape[0], o.shape[1], -1)


def attention_mixer(h_lat, h_ctx, w_in, w_out, sink, q_gain, k_gain, cos, sin, need_ctx):
    scale = HEAD_DIM ** -0.5
    qa, ka, va, qb, kb, vb = split_attn_projection(h_lat @ w_in)
    qa_c, ka_c, va_c, qb_c, kb_c, vb_c = split_attn_projection(h_ctx @ w_in)
    sink = sink.reshape(A_KV_HEADS, A_GROUP)
    qa = apply_rope(qa, cos, sin) * scale
    ka = apply_rope(ka, cos, sin)
    out_a = window_attention_latent(qa, ka, va, ka_c, va_c, sink)
    qb = apply_rope(rmsnorm(qb, q_gain), cos, sin) * scale
    kb = apply_rope(rmsnorm(kb, k_gain), cos, sin)
    kb_c = rmsnorm(kb_c, k_gain)
    out_b = dense_attention_latent(qb, jnp.concatenate([kb, kb_c], axis=1), jnp.concatenate([vb, vb_c], axis=1))
    y_lat = jnp.concatenate([out_a, out_b], axis=-1) @ w_out
    y_ctx = None
    if need_ctx:
        out_a_c = context_attention(qa_c * scale, ka_c, va_c, sink)
        out_b_c = context_attention(rmsnorm(qb_c, q_gain) * scale, kb_c, vb_c, None)
        y_ctx = jnp.concatenate([out_a_c, out_b_c], axis=-1) @ w_out
    return y_lat, y_ctx


def gla_project(h, w_in, gate_w, gate_b):
    bsz, length, _ = h.shape
    q, k, v, r, g_f, g_b = jnp.split(h @ w_in, GLA_SPLITS, axis=-1)

    def heads(t, d):
        return t.reshape(bsz, length, GLA_HEADS, d)

    def log_decay(g, i):
        return heads(jax.nn.log_sigmoid((g @ gate_w[i] + gate_b[i]).astype(jnp.float32)) / GLA_GATE_NORM, GLA_DK)

    return (heads(q, GLA_DK) * GLA_DK ** -0.5, heads(k, GLA_DK), heads(v, GLA_DV), heads(r, GLA_DV),
            log_decay(g_f, 0), log_decay(g_b, 1))


def gla_chunked(q, k, v, log_g, state0, with_output):
    bsz, length, nh, _ = q.shape
    dv = v.shape[-1]
    nc = length // GLA_CHUNK

    def to_chunks(t):
        return jnp.transpose(t.astype(jnp.float32).reshape(bsz, nc, GLA_CHUNK, nh, t.shape[-1]), (1, 0, 3, 2, 4))

    qc, kc, vc, gc = (to_chunks(t) for t in (q, k, v, log_g))
    b = jnp.cumsum(gc, axis=3)
    b_end = b[:, :, :, -1:, :]
    k_end = kc * jnp.exp(b_end - b)
    decay_end = jnp.exp(b[:, :, :, -1, :])
    if not with_output:
        def step_state(S, xs):
            ke, vv, de = xs
            return de[..., None] * S + jnp.einsum('bhcd,bhce->bhde', ke, vv), None
        s_fin, _ = lax.scan(step_state, state0, (k_end, vc, decay_end))
        return None, s_fin
    q_dec = qc * jnp.exp(b)

    def step(S, xs):
        qd, ke, vv, de = xs
        inter = jnp.einsum('bhcd,bhde->bhce', qd, S)
        return de[..., None] * S + jnp.einsum('bhcd,bhce->bhde', ke, vv), inter

    s_fin, inter = lax.scan(step, state0, (q_dec, k_end, vc, decay_end))
    lower = jnp.tril(jnp.ones((GLA_CHUNK, GLA_CHUNK), dtype=bool))
    att = jnp.einsum('nbhcd,nbhsd->nbhcs', q_dec, kc * jnp.exp(-b))
    intra = jnp.einsum('nbhcs,nbhse->nbhce', jnp.where(lower, att, 0.0), vc)
    out = jnp.transpose(inter + intra, (1, 0, 3, 2, 4)).reshape(bsz, length, nh, dv)
    return out.astype(v.dtype), s_fin


def gla_mixer(h_lat, h_ctx, w_in, gate_w, gate_b, head_gain, w_out, need_ctx):
    bsz = h_lat.shape[0]
    ql, kl, vl, rl, gfl, gbl = gla_project(h_lat, w_in, gate_w, gate_b)
    qc, kc, vc, rc, gfc, gbc = gla_project(h_ctx, w_in, gate_w, gate_b)
    s0 = jnp.zeros((bsz, GLA_HEADS, GLA_DK, GLA_DV), jnp.float32)

    def rev(t):
        return jnp.flip(t, axis=1)

    o_cf, s_f = gla_chunked(qc, kc, vc, gfc, s0, need_ctx)
    o_cb, s_b = gla_chunked(rev(qc), rev(kc), rev(vc), rev(gbc), s0, need_ctx)
    o_lf, _ = gla_chunked(ql, kl, vl, gfl, s_f, True)
    o_lb, _ = gla_chunked(rev(ql), rev(kl), rev(vl), rev(gbl), s_b, True)

    def finish(o, r):
        y = rmsnorm(o, head_gain) * jax.nn.silu(r)
        return y.reshape(bsz, y.shape[1], GLA_VD) @ w_out

    y_lat = finish(o_lf + rev(o_lb), rl)
    y_ctx = finish(o_cf + rev(o_cb), rc) if need_ctx else None
    return y_lat, y_ctx


def peer_ffn(h, w_query, sub_keys, u, v):
    shape = h.shape
    t = h.reshape(-1, shape[-1])
    n_tok = t.shape[0]
    q = (t @ w_query).reshape(n_tok, PEER_HEADS, 2, PEER_HALF)
    s = jnp.einsum('thpd,hpnd->thpn', q, sub_keys).astype(jnp.float32)
    s_top, i_top = lax.top_k(s, PEER_TOPK)
    cand_s = (s_top[:, :, 0, :, None] + s_top[:, :, 1, None, :]).reshape(n_tok, PEER_HEADS, PEER_TOPK * PEER_TOPK)
    cand_i = (i_top[:, :, 0, :, None] * PEER_N_KEYS + i_top[:, :, 1, None, :]).reshape(n_tok, PEER_HEADS, PEER_TOPK * PEER_TOPK)
    best_s, best_pos = lax.top_k(cand_s, PEER_TOPK)
    experts = jnp.take_along_axis(cand_i, best_pos, axis=-1)
    gates = jax.nn.softmax(best_s, axis=-1).astype(h.dtype)
    nb = n_tok // PEER_TOKEN_BLOCK

    def block(args):
        tb, eb, gb = args
        act = jax.nn.gelu(jnp.einsum('td,thkd->thk', tb, jnp.take(u, eb, axis=0)), approximate=False)
        return jnp.einsum('thk,thkd->td', gb * act, jnp.take(v, eb, axis=0))

    out = lax.map(block, (t.reshape(nb, PEER_TOKEN_BLOCK, shape[-1]),
                          experts.reshape(nb, PEER_TOKEN_BLOCK, PEER_HEADS, PEER_TOPK),
                          gates.reshape(nb, PEER_TOKEN_BLOCK, PEER_HEADS, PEER_TOPK)))
    return out.reshape(shape)


def setup_inputs(seed: int = 0) -> dict:
    key = jax.random.key(seed)
    ks = jax.random.split(key, 23)

    def nrm(k, shape, scale):
        return scale * jax.random.normal(k, shape, jnp.float32)

    def gain(k, shape):
        return 1.0 + 0.02 * jax.random.normal(k, shape, jnp.float32)

    return {
        'x': nrm(ks[0], (BATCH, SEQ, D_MODEL), 1.0),
        'c': nrm(ks[1], (BATCH, D_MODEL), 1.0),
        'ctx': nrm(ks[2], (BATCH, CTX_LEN, D_MODEL), 1.0),
        'c_ctx': nrm(ks[3], (D_MODEL,), 1.0),
        'ada_w': nrm(ks[4], (DEPTH, D_MODEL, 6 * D_MODEL), 0.5 * D_MODEL ** -0.5),
        'ada_b': nrm(ks[5], (DEPTH, 6 * D_MODEL), 0.02),
        'norm_mix': gain(ks[6], (DEPTH, D_MODEL)),
        'norm_ffn': gain(ks[7], (DEPTH, D_MODEL)),
        'attn_w_in': nrm(ks[8], (N_EVEN, D_MODEL, ATTN_IN_DIM), D_MODEL ** -0.5),
        'attn_sink': nrm(ks[9], (N_EVEN, A_Q_HEADS), 1.0),
        'attn_q_gain': gain(ks[10], (N_EVEN, HEAD_DIM)),
        'attn_k_gain': gain(ks[11], (N_EVEN, HEAD_DIM)),
        'attn_w_out': nrm(ks[12], (N_EVEN, ATTN_OUT_DIM, D_MODEL), ATTN_OUT_DIM ** -0.5),
        'gla_w_in': nrm(ks[13], (N_ODD, D_MODEL, GLA_IN_DIM), D_MODEL ** -0.5),
        'gla_gate_w': nrm(ks[14], (N_ODD, 2, GLA_RANK, GLA_KD), GLA_RANK ** -0.5),
        'gla_gate_b': nrm(ks[15], (N_ODD, 2, GLA_KD), 0.1),
        'gla_head_gain': gain(ks[16], (N_ODD, GLA_DV)),
        'gla_w_out': nrm(ks[17], (N_ODD, GLA_VD, D_MODEL), GLA_VD ** -0.5),
        'peer_wq': nrm(ks[18], (DEPTH, D_MODEL, PEER_HEADS * PEER_QUERY_DIM), D_MODEL ** -0.5),
        'peer_keys': nrm(ks[19], (DEPTH, PEER_HEADS, 2, PEER_N_KEYS, PEER_HALF), PEER_HALF ** -0.5),
        'peer_u': nrm(ks[20], (DEPTH, PEER_N_EXPERTS, D_MODEL), D_MODEL ** -0.5),
        'peer_v': nrm(ks[21], (DEPTH, PEER_N_EXPERTS, D_MODEL), 1.0),
        'final_norm': gain(ks[22], (D_MODEL,)),
    }


def reference(x, c, ctx, c_ctx, ada_w, ada_b, norm_mix, norm_ffn, attn_w_in, attn_sink, attn_q_gain,
              attn_k_gain, attn_w_out, gla_w_in, gla_gate_w, gla_gate_b, gla_head_gain, gla_w_out,
              peer_wq, peer_keys, peer_u, peer_v, final_norm):
    length = x.shape[1]
    rows = length // GRID_W
    cos, sin = axial_rope_tables(rows)
    x_lat, x_ctx = x, ctx
    for layer in range(DEPTH):
        need_ctx = layer < DEPTH - 1
        sh1, sc1, g1, sh2, sc2, g2 = adaln(c, ada_w[layer], ada_b[layer])
        csh1, csc1, cg1, csh2, csc2, cg2 = adaln(c_ctx[None], ada_w[layer], ada_b[layer])
        h_lat = rmsnorm(x_lat, norm_mix[layer]) * (1.0 + sc1) + sh1
        h_ctx = rmsnorm(x_ctx, norm_mix[layer]) * (1.0 + csc1) + csh1
        j = layer // 2
        if layer % 2 == 0:
            y_lat, y_ctx = attention_mixer(h_lat, h_ctx, attn_w_in[j], attn_w_out[j], attn_sink[j],
                                           attn_q_gain[j], attn_k_gain[j], cos, sin, need_ctx)
        else:
            y_lat, y_ctx = gla_mixer(h_lat, h_ctx, gla_w_in[j], gla_gate_w[j], gla_gate_b[j],
                                     gla_head_gain[j], gla_w_out[j], need_ctx)
        x_lat = x_lat + g1 * y_lat
        f_lat = rmsnorm(x_lat, norm_ffn[layer]) * (1.0 + sc2) + sh2
        x_lat = x_lat + g2 * peer_ffn(f_lat, peer_wq[layer], peer_keys[layer], peer_u[layer], peer_v[layer])
        if need_ctx:
            x_ctx = x_ctx + cg1 * y_ctx
            f_ctx = rmsnorm(x_ctx, norm_ffn[layer]) * (1.0 + csc2) + csh2
            x_ctx = x_ctx + cg2 * peer_ffn(f_ctx, peer_wq[layer], peer_keys[layer], peer_u[layer], peer_v[layer])
    return rmsnorm(x_lat, final_norm)
```

```python
import functools
import math

import jax
import jax.numpy as jnp
from jax import lax
from jax.experimental import pallas as pl
from jax.experimental.pallas import tpu as pltpu

F32 = jnp.float32
BF16 = jnp.bfloat16
I32 = jnp.int32

SUBLANES = 8
LANES = 128
VMEM_BYTES_V7X = 64 * 1024 * 1024

EPS = 1e-6
NEG_INF = -1e30
GRID_W = 64
WINDOW = 128
HEAD_DIM = 64
ROPE_THETA = 10000.0
ROPE_PAIRS = HEAD_DIM // 4
N_Q_HEADS = 8
GLA_HEADS = 4
GLA_GATE_NORM = 16.0
GLA_CHUNK = 64
PEER_HEADS = 8
PEER_N_KEYS = 128
PEER_TOPK = 16
PEER_PAIRS = PEER_HEADS * PEER_TOPK
ROW_BLOCK = 256
PEER_BLOCK = 128
KV_CHUNK = 768

NT_DIMS = (((1,), (1,)), ((), ()))


def _cparams(sem, vmem_mb):
    return pltpu.CompilerParams(dimension_semantics=sem, vmem_limit_bytes=vmem_mb * 1024 * 1024)


def _dot(a, b):
    return jnp.dot(a, b, preferred_element_type=F32)


def _dot_nt(a, b):
    return lax.dot_general(a, b, NT_DIMS, preferred_element_type=F32)


def _split_dot(a, b_bf16):
    hi = a.astype(BF16)
    lo = (a - hi.astype(F32)).astype(BF16)
    return _dot(hi, b_bf16) + _dot(lo, b_bf16)


def _rms_rows(x, gain):
    ms = jnp.mean(x * x, axis=-1, keepdims=True)
    return x * lax.rsqrt(ms + EPS) * gain


def _adaln_kernel(cond_ref, w_ref, b_ref, o_ref):
    cnd = cond_ref[...]
    act = cnd * (1.0 / (1.0 + jnp.exp(-cnd)))
    o_ref[...] = jnp.dot(act, w_ref[...], precision=lax.Precision.HIGHEST,
                         preferred_element_type=F32) + b_ref[...]


def _adaln(cond8, ada_w, ada_b):
    depth, d, n6 = ada_w.shape
    tn = n6 // 4
    return pl.pallas_call(
        _adaln_kernel,
        grid=(depth, 4),
        in_specs=[pl.BlockSpec((SUBLANES, d), lambda l, j: (0, 0)),
                  pl.BlockSpec((None, d, tn), lambda l, j: (l, 0, j)),
                  pl.BlockSpec((None, 1, tn), lambda l, j: (l, 0, j))],
        out_specs=pl.BlockSpec((None, SUBLANES, tn), lambda l, j: (l, 0, j)),
        out_shape=jax.ShapeDtypeStruct((depth, SUBLANES, n6), F32),
        compiler_params=_cparams(("arbitrary", "arbitrary"), 40),
        name="adaln",
    )(cond8, ada_w, ada_b.reshape(depth, 1, n6))


def _modulated_norm(x_ref, gain_ref, sc_ref, sh_ref):
    x = x_ref[...]
    return _rms_rows(x, gain_ref[...]) * (1.0 + sc_ref[...]) + sh_ref[...]


def _seg_spec(d):
    return pl.BlockSpec((None, None, 1, d), lambda b, i: (b, jnp.minimum(i, 1), 0, 0))


def _lat_spec(d):
    return pl.BlockSpec((None, None, 1, d), lambda b, i: (b, 1, 0, 0))


def _const_spec(shape):
    nd = len(shape)
    return pl.BlockSpec(shape, lambda b, i: (0,) * nd)


def _swap16(t, lane):
    return jnp.where((lane % 32) < 16, pltpu.roll(t, LANES - 16, 1), pltpu.roll(t, 16, 1))


def _attn_in_kernel(x_ref, gain_ref, sc_ref, sh_ref, w_ref, cos_ref, sin_ref, gmean_ref, qg_ref, kg_ref,
                    qa_ref, qb_ref, kva_ref, kvb_ref):
    h = _modulated_norm(x_ref, gain_ref, sc_ref, sh_ref).astype(BF16)
    p = _dot(h, w_ref[...])
    lane = lax.broadcasted_iota(I32, (1, LANES), 1)
    low = lane < HEAD_DIM
    cs, sn = cos_ref[...], sin_ref[...]
    scale = HEAD_DIM ** -0.5

    def rope(t):
        return t * cs + _swap16(t, lane) * sn

    def head_norm(t, g_ref):
        ms = _split_dot(t * t, gmean_ref[...])
        return t * lax.rsqrt(ms + EPS) * g_ref[...]

    def tile(j):
        return p[:, j * LANES:(j + 1) * LANES]

    def put_q(q_ref, j, t):
        tr = pltpu.roll(t, HEAD_DIM, 1)
        zero = jnp.zeros_like(t)
        if j // 2 == 0:
            even, odd = jnp.where(low, t, zero), jnp.where(low, tr, zero)
        else:
            even, odd = jnp.where(low, zero, tr), jnp.where(low, zero, t)
        q_ref[:, (2 * j) * LANES:(2 * j + 1) * LANES] = even.astype(BF16)
        q_ref[:, (2 * j + 1) * LANES:(2 * j + 2) * LANES] = odd.astype(BF16)

    for j in range(4):
        put_q(qa_ref, j, rope(tile(j)) * scale)
        put_q(qb_ref, j, rope(head_norm(tile(6 + j), qg_ref)) * scale)
    for kv_ref, k, v in ((kva_ref, rope(tile(4)), tile(5)),
                         (kvb_ref, rope(head_norm(tile(10), kg_ref)), tile(11))):
        kv_ref[:, 0:LANES] = k.astype(BF16)
        kv_ref[:, LANES:2 * LANES] = v.astype(BF16)
        kv_ref[:, 2 * LANES:3 * LANES] = pltpu.roll(v, HEAD_DIM, 1).astype(BF16)


def _attn_in(x_all, gain, sc, sh, w_bf16, cos_t, sin_t, gmean, qg, kg):
    bsz, s, d = x_all.shape
    n = w_bf16.shape[1]
    r = ROW_BLOCK
    row = lambda w: pl.BlockSpec((None, r, w), lambda b, i: (b, i, 0))
    tab = pl.BlockSpec((r, LANES), lambda b, i: (i, 0))
    return pl.pallas_call(
        _attn_in_kernel,
        grid=(bsz, s // r),
        in_specs=[row(d), _const_spec((1, d)), _seg_spec(d), _seg_spec(d), _const_spec((d, n)), tab, tab,
                  _const_spec((LANES, LANES)), _const_spec((1, LANES)), _const_spec((1, LANES))],
        out_specs=[row(N_Q_HEADS * LANES), row(N_Q_HEADS * LANES), row(3 * LANES), row(3 * LANES)],
        out_shape=[jax.ShapeDtypeStruct((bsz, s, N_Q_HEADS * LANES), BF16)] * 2
        + [jax.ShapeDtypeStruct((bsz, s, 3 * LANES), BF16)] * 2,
        compiler_params=_cparams(("parallel", "arbitrary"), 40),
        name="attn_in",
    )(x_all, gain, sc, sh, w_bf16, cos_t, sin_t, gmean, qg, kg)


def _value_tile(kvh, par, v_nat, v_swap):
    return v_nat if kvh == par else v_swap


def _attn_win_kernel(sink_ref, q_ref, cur_ref, prev_ref, next_ref, ctx_ref, o_ref, *, n_ctx, n_tok):
    i = pl.program_id(1)
    r = ROW_BLOCK
    kv = jnp.concatenate([prev_ref[...], cur_ref[...], next_ref[...], ctx_ref[...]], axis=0)
    k, v_nat, v_swap = kv[:, 0:LANES], kv[:, LANES:2 * LANES], kv[:, 2 * LANES:3 * LANES]
    n_band = 2 * r
    qtok = i * r + lax.broadcasted_iota(I32, (r, 1), 0)
    col = lax.broadcasted_iota(I32, (1, n_band + n_ctx), 1)
    ktok = i * r - WINDOW + col
    valid = (col >= n_band) | ((ktok >= n_ctx) & (ktok < n_tok) & (qtok >= n_ctx)
                               & (jnp.abs(qtok - ktok) <= WINDOW))
    low = lax.broadcasted_iota(I32, (1, LANES), 1) < HEAD_DIM
    for j in range(N_Q_HEADS // 2):
        outs = []
        for par in range(2):
            hq = 2 * j + par
            s = _dot_nt(q_ref[:, hq * LANES:(hq + 1) * LANES], k)
            s = jnp.where(valid, s, NEG_INF)
            sink = sink_ref[hq]
            m = jnp.maximum(jnp.max(s, axis=1, keepdims=True), sink)
            e = jnp.exp(s - m)
            den = jnp.sum(e, axis=1, keepdims=True) + jnp.exp(sink - m)
            pv = _dot(e.astype(BF16), _value_tile(j // 2, par, v_nat, v_swap))
            outs.append(pv / den)
        o_ref[:, j * LANES:(j + 1) * LANES] = jnp.where(low, outs[0], outs[1]).astype(BF16)


def _attn_win(sink, qa, kva, n_ctx):
    bsz, s, _ = qa.shape
    r = ROW_BLOCK
    half = r // 2
    last_half = s // half - 1
    kw = 3 * LANES
    return pl.pallas_call(
        functools.partial(_attn_win_kernel, n_ctx=n_ctx, n_tok=s),
        grid=(bsz, s // r),
        in_specs=[pl.BlockSpec(memory_space=pltpu.SMEM),
                  pl.BlockSpec((None, r, N_Q_HEADS * LANES), lambda b, i: (b, i, 0)),
                  pl.BlockSpec((None, r, kw), lambda b, i: (b, i, 0)),
                  pl.BlockSpec((None, half, kw), lambda b, i: (b, jnp.maximum(2 * i - 1, 0), 0)),
                  pl.BlockSpec((None, half, kw), lambda b, i: (b, jnp.minimum(2 * i + 2, last_half), 0)),
                  pl.BlockSpec((None, n_ctx, kw), lambda b, i: (b, 0, 0))],
        out_specs=pl.BlockSpec((None, r, N_Q_HEADS * HEAD_DIM), lambda b, i: (b, i, 0)),
        out_shape=jax.ShapeDtypeStruct((bsz, s, N_Q_HEADS * HEAD_DIM), BF16),
        compiler_params=_cparams(("parallel", "arbitrary"), 40),
        name="attn_win",
    )(sink, qa, kva, kva, kva, kva)


def _attn_dense_kernel(q_ref, kv_ref, o_ref, m_ref, l_ref, acc_ref, *, n_ctx, n_tok):
    i = pl.program_id(1)
    low = lax.broadcasted_iota(I32, (1, LANES), 1) < HEAD_DIM
    m_ref[...] = jnp.full(m_ref.shape, NEG_INF, F32)
    l_ref[...] = jnp.zeros(l_ref.shape, F32)
    acc_ref[...] = jnp.zeros(acc_ref.shape, F32)

    def chunk(start, size):
        kv = kv_ref[pl.ds(start, size), :]
        k, v_nat, v_swap = kv[:, 0:LANES], kv[:, LANES:2 * LANES], kv[:, 2 * LANES:3 * LANES]
        for j in range(N_Q_HEADS // 2):
            upd = []
            for par in range(2):
                hq = 2 * j + par
                s = _dot_nt(q_ref[:, hq * LANES:(hq + 1) * LANES], k)
                m_old = m_ref[hq]
                m_new = jnp.maximum(m_old, jnp.max(s, axis=1, keepdims=True))
                alpha = jnp.exp(m_old - m_new)
                e = jnp.exp(s - m_new)
                l_ref[hq] = alpha * l_ref[hq] + jnp.sum(e, axis=1, keepdims=True)
                m_ref[hq] = m_new
                pv = _dot(e.astype(BF16), _value_tile(j // 2, par, v_nat, v_swap))
                upd.append(alpha * acc_ref[j] + pv)
            acc_ref[j] = jnp.where(low, upd[0], upd[1])

    @pl.when(i == 0)
    def _():
        chunk(0, n_ctx)

    @pl.when(i > 0)
    def _():
        def body(c, carry):
            chunk(pl.multiple_of(c * KV_CHUNK, KV_CHUNK), KV_CHUNK)
            return carry
        lax.fori_loop(0, n_tok // KV_CHUNK, body, 0)

    for j in range(N_Q_HEADS // 2):
        inv = jnp.where(low, 1.0 / l_ref[2 * j], 1.0 / l_ref[2 * j + 1])
        o_ref[:, j * LANES:(j + 1) * LANES] = (acc_ref[j] * inv).astype(BF16)


def _attn_dense(qb, kvb, n_ctx):
    bsz, s, _ = qb.shape
    r = ROW_BLOCK
    assert s % KV_CHUNK == 0 and n_ctx == r
    return pl.pallas_call(
        functools.partial(_attn_dense_kernel, n_ctx=n_ctx, n_tok=s),
        grid=(bsz, s // r),
        in_specs=[pl.BlockSpec((None, r, N_Q_HEADS * LANES), lambda b, i: (b, i, 0)),
                  pl.BlockSpec((None, s, 3 * LANES), lambda b, i: (b, 0, 0))],
        out_specs=pl.BlockSpec((None, r, N_Q_HEADS * HEAD_DIM), lambda b, i: (b, i, 0)),
        out_shape=jax.ShapeDtypeStruct((bsz, s, N_Q_HEADS * HEAD_DIM), BF16),
        scratch_shapes=[pltpu.VMEM((N_Q_HEADS, r, 1), F32), pltpu.VMEM((N_Q_HEADS, r, 1), F32),
                        pltpu.VMEM((N_Q_HEADS // 2, r, LANES), F32)],
        compiler_params=_cparams(("parallel", "arbitrary"), 48),
        name="attn_dense",
    )(qb, kvb)


def _residual_ffn_norm(x_ref, y, g1_ref, gain_ref, sc_ref, sh_ref, xo_ref, f_ref):
    x_new = x_ref[...] + g1_ref[...] * y
    xo_ref[...] = x_new
    f_ref[...] = _rms_rows(x_new, gain_ref[...]) * (1.0 + sc_ref[...]) + sh_ref[...]


def _attn_out_kernel(x_ref, oa_ref, ob_ref, wa_ref, wb_ref, g1_ref, gain_ref, sc_ref, sh_ref, xo_ref, f_ref):
    y = _dot(oa_ref[...], wa_ref[...]) + _dot(ob_ref[...], wb_ref[...])
    _residual_ffn_norm(x_ref, y, g1_ref, gain_ref, sc_ref, sh_ref, xo_ref, f_ref)


def _attn_out(x_all, oa, ob, wa, wb, g1, gain, sc, sh):
    bsz, s, d = x_all.shape
    r = ROW_BLOCK
    row = lambda w: pl.BlockSpec((None, r, w), lambda b, i: (b, i, 0))
    return pl.pallas_call(
        _attn_out_kernel,
        grid=(bsz, s // r),
        in_specs=[row(d), row(oa.shape[2]), row(ob.shape[2]), _const_spec(wa.shape), _const_spec(wb.shape),
                  _seg_spec(d), _const_spec((1, d)), _seg_spec(d), _seg_spec(d)],
        out_specs=[row(d), row(d)],
        out_shape=[jax.ShapeDtypeStruct((bsz, s, d), F32)] * 2,
        compiler_params=_cparams(("parallel", "arbitrary"), 40),
        name="attn_out",
    )(x_all, oa, ob, wa, wb, g1, gain, sc, sh)


def _gla_out_kernel(x_ref, of_ref, ob_ref, r_ref, hg_ref, w_ref, g1_ref, gain_ref, sc_ref, sh_ref, xo_ref, f_ref):
    o = of_ref[...] + ob_ref[...]
    rr = r_ref[...]
    dv = o.shape[1] // GLA_HEADS
    y = None
    for h in range(GLA_HEADS):
        sl = slice(h * dv, (h + 1) * dv)
        rh = rr[:, sl]
        t = _rms_rows(o[:, sl], hg_ref[...]) * (rh * (1.0 / (1.0 + jnp.exp(-rh))))
        part = _dot(t.astype(BF16), w_ref[sl, :])
        y = part if y is None else y + part
    _residual_ffn_norm(x_ref, y, g1_ref, gain_ref, sc_ref, sh_ref, xo_ref, f_ref)


def _gla_out(x_all, o_f, o_b, r_all, head_gain, w_out, g1, gain, sc, sh, n_ctx):
    bsz, s, d = x_all.shape
    r = ROW_BLOCK
    skip = n_ctx // r
    row_in = lambda w: pl.BlockSpec((None, r, w), lambda b, i: (b, i + skip, 0))
    row_out = pl.BlockSpec((None, r, d), lambda b, i: (b, i, 0))
    return pl.pallas_call(
        _gla_out_kernel,
        grid=(bsz, (s - n_ctx) // r),
        in_specs=[row_in(d), row_in(d), row_in(d), row_in(d), _const_spec(head_gain.shape), _const_spec(w_out.shape),
                  _lat_spec(d), _const_spec((1, d)), _lat_spec(d), _lat_spec(d)],
        out_specs=[row_out, row_out],
        out_shape=[jax.ShapeDtypeStruct((bsz, s - n_ctx, d), F32)] * 2,
        compiler_params=_cparams(("parallel", "arbitrary"), 40),
        name="gla_out",
    )(x_all, o_f, o_b, r_all, head_gain, w_out, g1, gain, sc, sh)


def _peer_route_kernel(f_ref, wq_ref, keys_ref, gates_ref, experts_ref, q_ref, s_top_ref, i_top_ref):
    r = f_ref.shape[0]
    k16 = PEER_TOPK
    q_ref[...] = _dot(f_ref[...].astype(BF16), wq_ref[...]).astype(BF16)
    key_id = lax.broadcasted_iota(I32, (PEER_N_KEYS, r), 0).astype(F32)
    minus_inf = jnp.float32(-jnp.inf)

    def sub_key_topk(hp, carry):
        qs = q_ref[:, pl.ds(pl.multiple_of(hp * PEER_N_KEYS, PEER_N_KEYS), PEER_N_KEYS)]
        scores = _dot_nt(keys_ref[hp], qs)

        def take(k, v):
            m = jnp.max(v, axis=0, keepdims=True)
            idx = jnp.min(jnp.where(v == m, key_id, float(PEER_N_KEYS)), axis=0, keepdims=True)
            s_top_ref[hp, pl.ds(k, 1), :] = m
            i_top_ref[hp, pl.ds(k, 1), :] = idx
            return jnp.where(key_id == idx, minus_inf, v)

        lax.fori_loop(0, k16, take, scores)
        return carry

    lax.fori_loop(0, 2 * PEER_HEADS, sub_key_topk, 0)

    row8 = lax.broadcasted_iota(I32, (SUBLANES, r), 0)
    n_blocks = k16 + 1
    pos = jnp.concatenate([(row8 + (a_blk * SUBLANES if a_blk < 2 else (a_blk - 1) * k16)).astype(F32)
                           for a_blk in range(n_blocks)], axis=0)

    def product_topk(h, carry):
        s1, s2 = s_top_ref[2 * h], s_top_ref[2 * h + 1]
        i1, i2 = i_top_ref[2 * h], i_top_ref[2 * h + 1]
        cand, ids = [], []
        for blk in range(n_blocks):
            a = 0 if blk < 2 else blk - 1
            lo = SUBLANES if blk == 1 else 0
            c = s1[a:a + 1, :] + s2[lo:lo + SUBLANES, :]
            if a > 0:
                c = jnp.where(row8 < k16 // (a + 1), c, minus_inf)
            cand.append(c)
            ids.append(i1[a:a + 1, :] * PEER_N_KEYS + i2[lo:lo + SUBLANES, :])
        cand = jnp.concatenate(cand, axis=0)
        ids = jnp.concatenate(ids, axis=0)
        best = []
        for k in range(k16):
            m = jnp.max(cand, axis=0, keepdims=True)
            sel = jnp.min(jnp.where(cand == m, pos, float(k16 * k16)), axis=0, keepdims=True)
            hit = pos == sel
            expert = jnp.max(jnp.where(hit, ids, -1.0), axis=0, keepdims=True)
            experts_ref[pl.ds(h * k16 + k, 1), :] = expert.astype(I32)
            cand = jnp.where(hit, minus_inf, cand)
            best.append(m)
        e = [jnp.exp(b - best[0]) for b in best]
        den = functools.reduce(lambda x, y: x + y, e)
        for k in range(k16):
            gates_ref[pl.ds(h * k16 + k, 1), :] = e[k] / den
        return carry

    lax.fori_loop(0, PEER_HEADS, product_topk, 0)


def _peer_route(f_flat, wq_bf16, keys_bf16):
    t, d = f_flat.shape
    r = ROW_BLOCK
    nq = wq_bf16.shape[1]
    out = pl.BlockSpec((PEER_PAIRS, r), lambda i: (0, i))
    return pl.pallas_call(
        _peer_route_kernel,
        grid=(t // r,),
        in_specs=[pl.BlockSpec((r, d), lambda i: (i, 0)), pl.BlockSpec((d, nq), lambda i: (0, 0)),
                  pl.BlockSpec(keys_bf16.shape, lambda i: (0, 0, 0))],
        out_specs=[out, out],
        out_shape=[jax.ShapeDtypeStruct((PEER_PAIRS, t), F32), jax.ShapeDtypeStruct((PEER_PAIRS, t), I32)],
        scratch_shapes=[pltpu.VMEM((r, nq), BF16),
                        pltpu.VMEM((2 * PEER_HEADS, PEER_TOPK, r), F32),
                        pltpu.VMEM((2 * PEER_HEADS, PEER_TOPK, r), F32)],
        compiler_params=_cparams(("parallel",), 40),
        name="peer_route",
    )(f_flat, wq_bf16, keys_bf16)


def _pack_table(tbl):
    e, d = tbl.shape
    bits = lax.bitcast_convert_type(tbl.astype(BF16), jnp.uint16).astype(jnp.uint32).reshape(e // 2, 2, d)
    words = (bits[:, 0, :] << 16) | bits[:, 1, :]
    return words.reshape(e // 2 * (d // LANES), LANES)


def _expert_row(tbl_ref, row8, shift):
    w = tbl_ref[pl.ds(pl.multiple_of(row8, SUBLANES), SUBLANES), :]
    return pltpu.bitcast((w << shift.astype(jnp.uint32)) & jnp.uint32(0xFFFF0000), F32)


def _fold8(vs, row):
    vs = [vs[n] for n in (0, 4, 2, 6, 1, 5, 3, 7)]
    step = SUBLANES // 2
    while len(vs) > 1:
        keep = (row % (2 * step)) < step
        nxt = []
        for a, b in zip(vs[0::2], vs[1::2]):
            if 2 * step == SUBLANES:
                nxt.append(jnp.where(keep, a, b) + pltpu.roll(jnp.where(keep, b, a), step, 0))
            else:
                nxt.append(jnp.where(keep, a + pltpu.roll(a, SUBLANES - step, 0), b + pltpu.roll(b, step, 0)))
        vs, step = nxt, step // 2
    return vs[0]


def _index_stream(hbm_refs, smem_refs, sem_ref, i, n_steps, width):
    slot = i % 2

    def copies(step, slot_):
        return [pltpu.make_async_copy(h.at[pl.ds(pl.multiple_of(step * width, width), width)],
                                      s.at[pl.ds(pl.multiple_of(slot_ * width, width), width)],
                                      sem_ref.at[n, slot_])
                for n, (h, s) in enumerate(zip(hbm_refs, smem_refs))]

    @pl.when(i == 0)
    def _():
        for c in copies(0, 0):
            c.start()

    for c in copies(i, slot):
        c.wait()

    @pl.when(i + 1 < n_steps)
    def _():
        for c in copies(i + 1, 1 - slot):
            c.start()

    return slot * width


def _load_table_once(tbl_hbm, tbl_ref, sem_ref, i):
    @pl.when(i == 0)
    def _():
        c = pltpu.make_async_copy(tbl_hbm, tbl_ref, sem_ref.at[0])
        c.start()
        c.wait()


def _erf(x):
    x = jnp.clip(x, -4.0, 4.0)
    z = x * x
    p = -2.72614225801306e-10
    for c in (2.77068142495902e-08, -2.10102402082508e-06, -5.69250639462346e-05, -7.34990630326855e-04,
              -2.95459980854025e-03, -1.60960333262415e-02):
        p = p * z + c
    q = -1.45660718464996e-05
    for c in (-2.13374055278905e-04, -1.68282697438203e-03, -7.37332916720468e-03, -1.42647390514189e-02):
        q = q * z + c
    return x * p / q


def _peer_up_kernel(row_hbm, shift_hbm, tbl_hbm, x_ref, gate_ref, o_ref,
                    row_smem, shift_smem, tbl_ref, part_ref, isem, tsem):
    i = pl.program_id(0)
    tb = x_ref.shape[0]
    width = tb * PEER_PAIRS
    _load_table_once(tbl_hbm, tbl_ref, tsem, i)
    base = _index_stream((row_hbm, shift_hbm), (row_smem, shift_smem), isem, i, pl.num_programs(0), width)
    row = lax.broadcasted_iota(I32, (SUBLANES, LANES), 0)

    def token(t, carry):
        xt = x_ref[t]
        off = base + t * PEER_PAIRS
        for g in range(PEER_PAIRS // SUBLANES):
            prods = []
            for s in range(SUBLANES):
                n = off + g * SUBLANES + s
                prods.append(_expert_row(tbl_ref, row_smem[n], shift_smem[n]) * xt)
            part_ref[pl.ds(pl.multiple_of(t * PEER_PAIRS + g * SUBLANES, SUBLANES), SUBLANES), :] = _fold8(prods, row)
        return carry

    lax.fori_loop(0, tb, token, 0)

    ones = jnp.ones((SUBLANES, LANES), BF16)

    def lane_sums(g, carry):
        a = part_ref[pl.ds(pl.multiple_of(g * SUBLANES * PEER_PAIRS, SUBLANES * PEER_PAIRS), SUBLANES * PEER_PAIRS), :]
        hi = a.astype(BF16)
        lo = (a - hi.astype(F32)).astype(BF16)
        sums = _dot_nt(ones, hi) + _dot_nt(ones, lo)
        for k in range(SUBLANES):
            o_ref[pl.ds(g * SUBLANES + k, 1), :] = sums[0:1, k * PEER_PAIRS:(k + 1) * PEER_PAIRS]
        return carry

    lax.fori_loop(0, tb // SUBLANES, lane_sums, 0)
    act = o_ref[...]
    o_ref[...] = gate_ref[...] * (0.5 * act * (1.0 + _erf(act * (2.0 ** -0.5))))


def _peer_up(rows, shifts, tbl_packed, f_tiles, gates):
    t = f_tiles.shape[0]
    tb = PEER_BLOCK
    width = tb * PEER_PAIRS
    return pl.pallas_call(
        _peer_up_kernel,
        grid=(t // tb,),
        in_specs=[pl.BlockSpec(memory_space=pl.ANY), pl.BlockSpec(memory_space=pl.ANY),
                  pl.BlockSpec(memory_space=pl.ANY),
                  pl.BlockSpec((tb, SUBLANES, LANES), lambda i: (i, 0, 0)),
                  pl.BlockSpec((tb, PEER_PAIRS), lambda i: (i, 0))],
        out_specs=pl.BlockSpec((tb, PEER_PAIRS), lambda i: (i, 0)),
        out_shape=jax.ShapeDtypeStruct((t, PEER_PAIRS), F32),
        scratch_shapes=[pltpu.SMEM((2 * width,), I32), pltpu.SMEM((2 * width,), I32),
                        pltpu.VMEM(tbl_packed.shape, jnp.uint32),
                        pltpu.VMEM((tb * PEER_PAIRS, LANES), F32),
                        pltpu.SemaphoreType.DMA((2, 2)), pltpu.SemaphoreType.DMA((1,))],
        compiler_params=_cparams(("arbitrary",), 52),
        name="peer_up",
    )(rows, shifts, tbl_packed, f_tiles, gates)


def _peer_down_kernel(row_hbm, shift_hbm, coef_hbm, tbl_hbm, x_ref, g2_ref, o_ref,
                      row_smem, shift_smem, coef_smem, tbl_ref, isem, tsem):
    i = pl.program_id(0)
    tb = x_ref.shape[0]
    width = tb * PEER_PAIRS
    _load_table_once(tbl_hbm, tbl_ref, tsem, i)
    base = _index_stream((row_hbm, shift_hbm, coef_hbm), (row_smem, shift_smem, coef_smem), isem, i,
                         pl.num_programs(0), width)
    g2 = g2_ref[...]
    n_acc = 4

    def token(t, carry):
        off = base + t * PEER_PAIRS
        accs = [jnp.zeros((SUBLANES, LANES), F32)] * n_acc
        for p in range(PEER_PAIRS):
            n = off + p
            accs[p % n_acc] = accs[p % n_acc] + _expert_row(tbl_ref, row_smem[n], shift_smem[n]) * coef_smem[n]
        o_ref[t] = x_ref[t] + g2 * ((accs[0] + accs[1]) + (accs[2] + accs[3]))
        return carry

    lax.fori_loop(0, tb, token, 0)


def _peer_down(rows, shifts, coefs, tbl_packed, x_tiles, g2_tiles, g2_index):
    t = x_tiles.shape[0]
    tb = PEER_BLOCK
    width = tb * PEER_PAIRS
    tok = pl.BlockSpec((tb, SUBLANES, LANES), lambda i: (i, 0, 0))
    return pl.pallas_call(
        _peer_down_kernel,
        grid=(t // tb,),
        in_specs=[pl.BlockSpec(memory_space=pl.ANY)] * 4
        + [tok, pl.BlockSpec((None, SUBLANES, LANES), lambda i: (g2_index(i), 0, 0))],
        out_specs=tok,
        out_shape=jax.ShapeDtypeStruct(x_tiles.shape, F32),
        scratch_shapes=[pltpu.SMEM((2 * width,), I32), pltpu.SMEM((2 * width,), I32), pltpu.SMEM((2 * width,), F32),
                        pltpu.VMEM(tbl_packed.shape, jnp.uint32),
                        pltpu.SemaphoreType.DMA((3, 2)), pltpu.SemaphoreType.DMA((1,))],
        compiler_params=_cparams(("arbitrary",), 52),
        name="peer_down",
    )(rows, shifts, coefs, tbl_packed, x_tiles, g2_tiles)


def _peer_ffn(f, x_res, g2_tiles, g2_index, wq_bf16, keys_bf16, u_packed, v_packed):
    t, d = f.shape
    gates_t, experts_t = _peer_route(f, wq_bf16, keys_bf16)
    experts = experts_t.T.reshape(-1)
    rows = (experts >> 1) * SUBLANES
    shifts = (experts & 1) * 16
    act = _peer_up(rows, shifts, u_packed, f.reshape(t, SUBLANES, LANES), gates_t.T)
    out = _peer_down(rows, shifts, act.reshape(-1), v_packed, x_res.reshape(t, SUBLANES, LANES), g2_tiles, g2_index)
    return out.reshape(t, d)


def _gla_in_kernel(x_ref, gain_ref, sc_ref, sh_ref, w_ref, wg_ref, gw_ref, gb_ref,
                   q_ref, k_ref, v_ref, r_ref, lgf_ref, lgb_ref):
    h = _modulated_norm(x_ref, gain_ref, sc_ref, sh_ref).astype(BF16)
    kd = q_ref.shape[1]
    vd = v_ref.shape[1]
    dk = kd // GLA_HEADS
    q_ref[...] = _dot(h, w_ref[:, 0:kd]) * dk ** -0.5
    k_ref[...] = _dot(h, w_ref[:, kd:2 * kd])
    v_ref[...] = _dot(h, w_ref[:, 2 * kd:2 * kd + vd]).astype(BF16)
    r_ref[...] = _dot(h, w_ref[:, 2 * kd + vd:2 * kd + 2 * vd])
    low_rank = _dot(h, wg_ref[...]).astype(BF16)
    z = _dot(low_rank, gw_ref[...]) + gb_ref[...]
    log_sig = jnp.minimum(z, 0.0) - jnp.log(1.0 + jnp.exp(-jnp.abs(z)))
    lgf_ref[...] = log_sig[:, 0:kd] / GLA_GATE_NORM
    lgb_ref[...] = log_sig[:, kd:2 * kd] / GLA_GATE_NORM


def _gla_in(x_all, gain, sc, sh, w_main, w_gate, gate_w, gate_b, kd, vd):
    bsz, s, d = x_all.shape
    r = ROW_BLOCK
    row = lambda w: pl.BlockSpec((None, r, w), lambda b, i: (b, i, 0))
    shp = lambda w, dt: jax.ShapeDtypeStruct((bsz, s, w), dt)
    return pl.pallas_call(
        _gla_in_kernel,
        grid=(bsz, s // r),
        in_specs=[row(d), _const_spec((1, d)), _seg_spec(d), _seg_spec(d), _const_spec(w_main.shape),
                  _const_spec(w_gate.shape), _const_spec(gate_w.shape), _const_spec(gate_b.shape)],
        out_specs=[row(kd), row(kd), row(vd), row(vd), row(kd), row(kd)],
        out_shape=[shp(kd, F32), shp(kd, F32), shp(vd, BF16), shp(vd, F32), shp(kd, F32), shp(kd, F32)],
        compiler_params=_cparams(("parallel", "arbitrary"), 48),
        name="gla_in",
    )(x_all, gain, sc, sh, w_main, w_gate, gate_w, gate_b)


def _gla_scan_kernel(qf_ref, kf_ref, vf_ref, gf_ref, qb_ref, kb_ref, vb_ref, gb_ref, of_ref, ob_ref,
                     sf_ref, sb_ref):
    i = pl.program_id(1)
    c = GLA_CHUNK
    dk = qf_ref.shape[1] // GLA_HEADS
    dv = vf_ref.shape[1] // GLA_HEADS
    n_chunks = qf_ref.shape[0] // c

    @pl.when(i == 0)
    def _():
        sf_ref[...] = jnp.zeros(sf_ref.shape, F32)
        sb_ref[...] = jnp.zeros(sb_ref.shape, F32)

    t_row = lax.broadcasted_iota(I32, (c, c), 0)
    t_col = lax.broadcasted_iota(I32, (c, c), 1)
    tri = {False: t_col <= t_row, True: t_col >= t_row}
    tri_bf16 = {rev: jnp.where(m, 1.0, 0.0).astype(BF16) for rev, m in tri.items()}

    def chunk(q_ref, k_ref, v_ref, g_ref, o_ref, s_ref, n, rev):
        rows = slice(n * c, (n + 1) * c)
        for h in range(GLA_HEADS):
            kl = slice(h * dk, (h + 1) * dk)
            vl = slice(h * dv, (h + 1) * dv)
            g = g_ref[rows, kl]
            b = _split_dot_left(tri_bf16[rev], g)
            total = jnp.sum(g, axis=0, keepdims=True)
            q_dec = (q_ref[rows, kl] * jnp.exp(b)).astype(BF16)
            kk = k_ref[rows, kl]
            k_end = kk * jnp.exp(total - b)
            k_neg = (kk * jnp.exp(-b)).astype(BF16)
            vv = v_ref[rows, vl]
            state = s_ref[h]
            inter = _dot(q_dec, state.astype(BF16))
            att = jnp.where(tri[rev], _dot_nt(q_dec, k_neg), 0.0)
            o_ref[rows, vl] = inter + _dot(att.astype(BF16), vv)
            decay_col = jnp.exp(jnp.sum(g.T, axis=1, keepdims=True))
            s_ref[h] = decay_col * state + _dot(k_end.T.astype(BF16), vv)

    for n in range(n_chunks):
        chunk(qf_ref, kf_ref, vf_ref, gf_ref, of_ref, sf_ref, n, False)
        chunk(qb_ref, kb_ref, vb_ref, gb_ref, ob_ref, sb_ref, n_chunks - 1 - n, True)


def _split_dot_left(a_bf16, b):
    hi = b.astype(BF16)
    lo = (b - hi.astype(F32)).astype(BF16)
    return _dot(a_bf16, hi) + _dot(a_bf16, lo)


def _gla_scan(q, k, v, lgf, lgb):
    bsz, s, kd = q.shape
    vd = v.shape[2]
    r = ROW_BLOCK
    n = s // r
    fwd = lambda w: pl.BlockSpec((None, r, w), lambda b, i: (b, i, 0))
    bwd = lambda w: pl.BlockSpec((None, r, w), lambda b, i: (b, jnp.where(i == 0, 0, n - i), 0))
    return pl.pallas_call(
        _gla_scan_kernel,
        grid=(bsz, n),
        in_specs=[fwd(kd), fwd(kd), fwd(vd), fwd(kd), bwd(kd), bwd(kd), bwd(vd), bwd(kd)],
        out_specs=[fwd(vd), bwd(vd)],
        out_shape=[jax.ShapeDtypeStruct((bsz, s, vd), F32)] * 2,
        scratch_shapes=[pltpu.VMEM((GLA_HEADS, kd // GLA_HEADS, vd // GLA_HEADS), F32)] * 2,
        compiler_params=_cparams(("parallel", "arbitrary"), 40),
        name="gla_scan",
    )(q, k, v, lgf, q, k, v, lgb)


def _final_norm_kernel(x_ref, gain_ref, o_ref):
    o_ref[...] = _rms_rows(x_ref[...], gain_ref[...])


def _final_norm(x, gain):
    t, d = x.shape
    r = 2 * ROW_BLOCK
    return pl.pallas_call(
        _final_norm_kernel,
        grid=(t // r,),
        in_specs=[pl.BlockSpec((r, d), lambda i: (i, 0)), pl.BlockSpec((1, d), lambda i: (0, 0))],
        out_specs=pl.BlockSpec((r, d), lambda i: (i, 0)),
        out_shape=jax.ShapeDtypeStruct((t, d), F32),
        compiler_params=_cparams(("parallel",), 40),
        name="final_norm",
    )(x, gain)


def _rope_tables(n_ctx, length):
    rows = length // GRID_W
    row = jnp.repeat(jnp.arange(rows), GRID_W)
    col = jnp.tile(jnp.arange(GRID_W), rows)
    pos = jnp.stack([row, col], axis=-1).astype(F32)
    inv = ROPE_THETA ** (-jnp.arange(ROPE_PAIRS, dtype=F32) / ROPE_PAIRS)
    ang = pos[:, :, None] * inv
    cos, sin = jnp.cos(ang), jnp.sin(ang)
    cos64 = jnp.concatenate([cos[:, 0], cos[:, 0], cos[:, 1], cos[:, 1]], axis=-1)
    sin64 = jnp.concatenate([-sin[:, 0], sin[:, 0], -sin[:, 1], sin[:, 1]], axis=-1)
    cos_t = jnp.concatenate([jnp.ones((n_ctx, LANES), F32), jnp.tile(cos64, (1, 2))], axis=0)
    sin_t = jnp.concatenate([jnp.zeros((n_ctx, LANES), F32), jnp.tile(sin64, (1, 2))], axis=0)
    return cos_t, sin_t


def kernel(x, c, ctx, c_ctx, ada_w, ada_b, norm_mix, norm_ffn, attn_w_in, attn_sink, attn_q_gain, attn_k_gain,
           attn_w_out, gla_w_in, gla_gate_w, gla_gate_b, gla_head_gain, gla_w_out, peer_wq, peer_keys, peer_u,
           peer_v, final_norm):
    bsz, length, d = x.shape
    n_ctx = ctx.shape[1]
    s = n_ctx + length
    assert n_ctx == ROW_BLOCK and d == SUBLANES * LANES and length % ROW_BLOCK == 0

    cond8 = jnp.zeros((SUBLANES, d), F32).at[:bsz].set(c).at[bsz].set(c_ctx)
    mod = _adaln(cond8, ada_w, ada_b)

    def seg_vectors(layer):
        lat = mod[layer, :bsz].reshape(bsz, 6, d)
        cx = jnp.broadcast_to(mod[layer, bsz].reshape(1, 6, d), (bsz, 6, d))
        both = jnp.stack([cx, lat], axis=1)
        return [both[:, :, n, :].reshape(bsz, 2, 1, d) for n in range(6)]

    def gate_tiles(g):
        return g.reshape(bsz * 2, SUBLANES, LANES)

    x_all = jnp.concatenate([ctx, x], axis=1)
    row = lambda v: v.reshape(1, -1)

    sh1, sc1, g1, sh2, sc2, g2 = seg_vectors(0)
    cos_t, sin_t = _rope_tables(n_ctx, length)
    lane = jnp.arange(LANES)
    gmean = jnp.where((lane[:, None] // HEAD_DIM) == (lane[None, :] // HEAD_DIM), 1.0 / HEAD_DIM, 0.0).astype(BF16)
    qa, qb, kva, kvb = _attn_in(x_all, row(norm_mix[0]), sc1, sh1, attn_w_in[0].astype(BF16), cos_t, sin_t, gmean,
                                row(jnp.tile(attn_q_gain[0], 2)), row(jnp.tile(attn_k_gain[0], 2)))
    oa = _attn_win(attn_sink[0], qa, kva, n_ctx)
    ob = _attn_dense(qb, kvb, n_ctx)
    w_out = attn_w_out[0].astype(BF16)
    half = w_out.shape[0] // 2
    x_all, f_all = _attn_out(x_all, oa, ob, w_out[:half], w_out[half:], g1, row(norm_ffn[0]), sc2, sh2)
    blocks_per_batch = s // PEER_BLOCK
    ctx_blocks = n_ctx // PEER_BLOCK
    x_all = _peer_ffn(
        f_all.reshape(bsz * s, d), x_all.reshape(bsz * s, d), gate_tiles(g2),
        lambda i: 2 * (i // blocks_per_batch) + jnp.where(i % blocks_per_batch < ctx_blocks, 0, 1),
        peer_wq[0].astype(BF16), peer_keys[0].reshape(2 * PEER_HEADS, PEER_N_KEYS, -1).astype(BF16),
        _pack_table(peer_u[0]), _pack_table(peer_v[0])).reshape(bsz, s, d)

    sh1, sc1, g1, sh2, sc2, g2 = seg_vectors(1)
    w_in = gla_w_in[0]
    kd = gla_gate_w.shape[-1]
    vd = (w_in.shape[1] - 2 * kd - 2 * gla_gate_w.shape[-2]) // 2
    rank = gla_gate_w.shape[-2]
    n_main = 2 * kd + 2 * vd
    w_gate = jnp.zeros((d, LANES), F32).at[:, :2 * rank].set(w_in[:, n_main:]).astype(BF16)
    gate_w = (jnp.zeros((LANES, 2 * kd), F32).at[:rank, :kd].set(gla_gate_w[0, 0])
              .at[rank:2 * rank, kd:].set(gla_gate_w[0, 1])).astype(BF16)
    q, k, v, r_all, lgf, lgb = _gla_in(x_all, row(norm_mix[1]), sc1, sh1, w_in[:, :n_main].astype(BF16), w_gate,
                                       gate_w, gla_gate_b[0].reshape(1, 2 * kd), kd, vd)
    o_f, o_b = _gla_scan(q, k, v, lgf, lgb)
    x_lat, f_lat = _gla_out(x_all, o_f, o_b, r_all, row(gla_head_gain[0]), gla_w_out[0].astype(BF16),
                            g1, row(norm_ffn[1]), sc2, sh2, n_ctx)
    lat_blocks = length // PEER_BLOCK
    x_lat = _peer_ffn(
        f_lat.reshape(bsz * length, d), x_lat.reshape(bsz * length, d), gate_tiles(g2),
        lambda i: 2 * (i // lat_blocks) + 1,
        peer_wq[1].astype(BF16), peer_keys[1].reshape(2 * PEER_HEADS, PEER_N_KEYS, -1).astype(BF16),
        _pack_table(peer_u[1]), _pack_table(peer_v[1]))
    return _final_norm(x_lat, row(final_norm)).reshape(bsz, length, d)
```

```python
import functools
import math

import jax
import jax.numpy as jnp
from jax import lax
from jax.experimental import pallas as pl
from jax.experimental.pallas import tpu as pltpu

F32 = jnp.float32
BF16 = jnp.bfloat16
I32 = jnp.int32

SUBLANES = 8
LANES = 128
VMEM_BYTES_V7X = 64 * 1024 * 1024

EPS = 1e-6
NEG_INF = -1e30
GRID_W = 64
WINDOW = 128
HEAD_DIM = 64
ROPE_THETA = 10000.0
ROPE_PAIRS = HEAD_DIM // 4
N_Q_HEADS = 8
GLA_HEADS = 4
GLA_GATE_NORM = 16.0
GLA_CHUNK = 64
PEER_HEADS = 8
PEER_N_KEYS = 128
PEER_TOPK = 16
PEER_PAIRS = PEER_HEADS * PEER_TOPK
ROW_BLOCK = 256
PEER_BLOCK = 128
KV_CHUNK = 768

NT_DIMS = (((1,), (1,)), ((), ()))


def _cparams(sem, vmem_mb):
    return pltpu.CompilerParams(dimension_semantics=sem, vmem_limit_bytes=vmem_mb * 1024 * 1024)


def _dot(a, b):
    return jnp.dot(a, b, preferred_element_type=F32)


def _dot_nt(a, b):
    return lax.dot_general(a, b, NT_DIMS, preferred_element_type=F32)


def _split_dot(a, b_bf16):
    hi = a.astype(BF16)
    lo = (a - hi.astype(F32)).astype(BF16)
    return _dot(hi, b_bf16) + _dot(lo, b_bf16)


def _rms_rows(x, gain):
    ms = jnp.mean(x * x, axis=-1, keepdims=True)
    return x * lax.rsqrt(ms + EPS) * gain


def _adaln_kernel(cond_ref, w_ref, b_ref, o_ref):
    cnd = cond_ref[...]
    act = cnd * (1.0 / (1.0 + jnp.exp(-cnd)))
    o_ref[...] = jnp.dot(act, w_ref[...], precision=lax.Precision.HIGHEST,
                         preferred_element_type=F32) + b_ref[...]


def _adaln(cond8, ada_w, ada_b):
    depth, d, n6 = ada_w.shape
    tn = n6 // 4
    return pl.pallas_call(
        _adaln_kernel,
        grid=(depth, 4),
        in_specs=[pl.BlockSpec((SUBLANES, d), lambda l, j: (0, 0)),
                  pl.BlockSpec((None, d, tn), lambda l, j: (l, 0, j)),
                  pl.BlockSpec((None, 1, tn), lambda l, j: (l, 0, j))],
        out_specs=pl.BlockSpec((None, SUBLANES, tn), lambda l, j: (l, 0, j)),
        out_shape=jax.ShapeDtypeStruct((depth, SUBLANES, n6), F32),
        compiler_params=_cparams(("arbitrary", "arbitrary"), 40),
        name="adaln",
    )(cond8, ada_w, ada_b.reshape(depth, 1, n6))


def _modulated_norm(x_ref, gain_ref, sc_ref, sh_ref):
    x = x_ref[...]
    return _rms_rows(x, gain_ref[...]) * (1.0 + sc_ref[...]) + sh_ref[...]


def _seg_spec(d):
    return pl.BlockSpec((None, None, 1, d), lambda b, i: (b, jnp.minimum(i, 1), 0, 0))


def _lat_spec(d):
    return pl.BlockSpec((None, None, 1, d), lambda b, i: (b, 1, 0, 0))


def _const_spec(shape):
    nd = len(shape)
    return pl.BlockSpec(shape, lambda b, i: (0,) * nd)


def _swap16(t, lane):
    return jnp.where((lane % 32) < 16, pltpu.roll(t, LANES - 16, 1), pltpu.roll(t, 16, 1))


def _attn_in_kernel(x_ref, gain_ref, sc_ref, sh_ref, w_ref, cos_ref, sin_ref, gmean_ref, qg_ref, kg_ref,
                    qa_ref, qb_ref, kva_ref, kvb_ref):
    h = _modulated_norm(x_ref, gain_ref, sc_ref, sh_ref).astype(BF16)
    p = _dot(h, w_ref[...])
    lane = lax.broadcasted_iota(I32, (1, LANES), 1)
    low = lane < HEAD_DIM
    cs, sn = cos_ref[...], sin_ref[...]
    scale = HEAD_DIM ** -0.5

    def rope(t):
        return t * cs + _swap16(t, lane) * sn

    def head_norm(t, g_ref):
        ms = _split_dot(t * t, gmean_ref[...])
        return t * lax.rsqrt(ms + EPS) * g_ref[...]

    def tile(j):
        return p[:, j * LANES:(j + 1) * LANES]

    def put_q(q_ref, j, t):
        tr = pltpu.roll(t, HEAD_DIM, 1)
        zero = jnp.zeros_like(t)
        if j // 2 == 0:
            even, odd = jnp.where(low, t, zero), jnp.where(low, tr, zero)
        else:
            even, odd = jnp.where(low, zero, tr), jnp.where(low, zero, t)
        q_ref[:, (2 * j) * LANES:(2 * j + 1) * LANES] = even.astype(BF16)
        q_ref[:, (2 * j + 1) * LANES:(2 * j + 2) * LANES] = odd.astype(BF16)

    for j in range(4):
        put_q(qa_ref, j, rope(tile(j)) * scale)
        put_q(qb_ref, j, rope(head_norm(tile(6 + j), qg_ref)) * scale)
    for kv_ref, k, v in ((kva_ref, rope(tile(4)), tile(5)),
                         (kvb_ref, rope(head_norm(tile(10), kg_ref)), tile(11))):
        kv_ref[:, 0:LANES] = k.astype(BF16)
        kv_ref[:, LANES:2 * LANES] = v.astype(BF16)
        kv_ref[:, 2 * LANES:3 * LANES] = pltpu.roll(v, HEAD_DIM, 1).astype(BF16)


def _attn_in(x_all, gain, sc, sh, w_bf16, cos_t, sin_t, gmean, qg, kg):
    bsz, s, d = x_all.shape
    n = w_bf16.shape[1]
    r = ROW_BLOCK
    row = lambda w: pl.BlockSpec((None, r, w), lambda b, i: (b, i, 0))
    tab = pl.BlockSpec((r, LANES), lambda b, i: (i, 0))
    return pl.pallas_call(
        _attn_in_kernel,
        grid=(bsz, s // r),
        in_specs=[row(d), _const_spec((1, d)), _seg_spec(d), _seg_spec(d), _const_spec((d, n)), tab, tab,
                  _const_spec((LANES, LANES)), _const_spec((1, LANES)), _const_spec((1, LANES))],
        out_specs=[row(N_Q_HEADS * LANES), row(N_Q_HEADS * LANES), row(3 * LANES), row(3 * LANES)],
        out_shape=[jax.ShapeDtypeStruct((bsz, s, N_Q_HEADS * LANES), BF16)] * 2
        + [jax.ShapeDtypeStruct((bsz, s, 3 * LANES), BF16)] * 2,
        compiler_params=_cparams(("parallel", "arbitrary"), 40),
        name="attn_in",
    )(x_all, gain, sc, sh, w_bf16, cos_t, sin_t, gmean, qg, kg)


def _value_tile(kvh, par, v_nat, v_swap):
    return v_nat if kvh == par else v_swap


def _attn_win_kernel(sink_ref, q_ref, cur_ref, prev_ref, next_ref, ctx_ref, o_ref, *, n_ctx, n_tok):
    i = pl.program_id(1)
    r = ROW_BLOCK
    kv = jnp.concatenate([prev_ref[...], cur_ref[...], next_ref[...], ctx_ref[...]], axis=0)
    k, v_nat, v_swap = kv[:, 0:LANES], kv[:, LANES:2 * LANES], kv[:, 2 * LANES:3 * LANES]
    n_band = 2 * r
    qtok = i * r + lax.broadcasted_iota(I32, (r, 1), 0)
    col = lax.broadcasted_iota(I32, (1, n_band + n_ctx), 1)
    ktok = i * r - WINDOW + col
    valid = (col >= n_band) | ((ktok >= n_ctx) & (ktok < n_tok) & (qtok >= n_ctx)
                               & (jnp.abs(qtok - ktok) <= WINDOW))
    low = lax.broadcasted_iota(I32, (1, LANES), 1) < HEAD_DIM
    for j in range(N_Q_HEADS // 2):
        outs = []
        for par in range(2):
            hq = 2 * j + par
            s = _dot_nt(q_ref[:, hq * LANES:(hq + 1) * LANES], k)
            s = jnp.where(valid, s, NEG_INF)
            sink = sink_ref[hq]
            m = jnp.maximum(jnp.max(s, axis=1, keepdims=True), sink)
            e = jnp.exp(s - m)
            den = jnp.sum(e, axis=1, keepdims=True) + jnp.exp(sink - m)
            pv = _dot(e.astype(BF16), _value_tile(j // 2, par, v_nat, v_swap))
            outs.append(pv / den)
        o_ref[:, j * LANES:(j + 1) * LANES] = jnp.where(low, outs[0], outs[1]).astype(BF16)


def _attn_win(sink, qa, kva, n_ctx):
    bsz, s, _ = qa.shape
    r = ROW_BLOCK
    half = r // 2
    last_half = s // half - 1
    kw = 3 * LANES
    return pl.pallas_call(
        functools.partial(_attn_win_kernel, n_ctx=n_ctx, n_tok=s),
        grid=(bsz, s // r),
        in_specs=[pl.BlockSpec(memory_space=pltpu.SMEM),
                  pl.BlockSpec((None, r, N_Q_HEADS * LANES), lambda b, i: (b, i, 0)),
                  pl.BlockSpec((None, r, kw), lambda b, i: (b, i, 0)),
                  pl.BlockSpec((None, half, kw), lambda b, i: (b, jnp.maximum(2 * i - 1, 0), 0)),
                  pl.BlockSpec((None, half, kw), lambda b, i: (b, jnp.minimum(2 * i + 2, last_half), 0)),
                  pl.BlockSpec((None, n_ctx, kw), lambda b, i: (b, 0, 0))],
        out_specs=pl.BlockSpec((None, r, N_Q_HEADS * HEAD_DIM), lambda b, i: (b, i, 0)),
        out_shape=jax.ShapeDtypeStruct((bsz, s, N_Q_HEADS * HEAD_DIM), BF16),
        compiler_params=_cparams(("parallel", "arbitrary"), 40),
        name="attn_win",
    )(sink, qa, kva, kva, kva, kva)


def _attn_dense_kernel(q_ref, kv_ref, o_ref, m_ref, l_ref, acc_ref, *, n_ctx, n_tok):
    i = pl.program_id(1)
    low = lax.broadcasted_iota(I32, (1, LANES), 1) < HEAD_DIM
    m_ref[...] = jnp.full(m_ref.shape, NEG_INF, F32)
    l_ref[...] = jnp.zeros(l_ref.shape, F32)
    acc_ref[...] = jnp.zeros(acc_ref.shape, F32)

    def chunk(start, size):
        kv = kv_ref[pl.ds(start, size), :]
        k, v_nat, v_swap = kv[:, 0:LANES], kv[:, LANES:2 * LANES], kv[:, 2 * LANES:3 * LANES]
        for j in range(N_Q_HEADS // 2):
            upd = []
            for par in range(2):
                hq = 2 * j + par
                s = _dot_nt(q_ref[:, hq * LANES:(hq + 1) * LANES], k)
                m_old = m_ref[hq]
                m_new = jnp.maximum(m_old, jnp.max(s, axis=1, keepdims=True))
                alpha = jnp.exp(m_old - m_new)
                e = jnp.exp(s - m_new)
                l_ref[hq] = alpha * l_ref[hq] + jnp.sum(e, axis=1, keepdims=True)
                m_ref[hq] = m_new
                pv = _dot(e.astype(BF16), _value_tile(j // 2, par, v_nat, v_swap))
                upd.append(alpha * acc_ref[j] + pv)
            acc_ref[j] = jnp.where(low, upd[0], upd[1])

    @pl.when(i == 0)
    def _():
        chunk(0, n_ctx)

    @pl.when(i > 0)
    def _():
        def body(c, carry):
            chunk(pl.multiple_of(c * KV_CHUNK, KV_CHUNK), KV_CHUNK)
            return carry
        lax.fori_loop(0, n_tok // KV_CHUNK, body, 0)

    for j in range(N_Q_HEADS // 2):
        inv = jnp.where(low, 1.0 / l_ref[2 * j], 1.0 / l_ref[2 * j + 1])
        o_ref[:, j * LANES:(j + 1) * LANES] = (acc_ref[j] * inv).astype(BF16)


def _attn_dense(qb, kvb, n_ctx):
    bsz, s, _ = qb.shape
    r = ROW_BLOCK
    assert s % KV_CHUNK == 0 and n_ctx == r
    return pl.pallas_call(
        functools.partial(_attn_dense_kernel, n_ctx=n_ctx, n_tok=s),
        grid=(bsz, s // r),
        in_specs=[pl.BlockSpec((None, r, N_Q_HEADS * LANES), lambda b, i: (b, i, 0)),
                  pl.BlockSpec((None, s, 3 * LANES), lambda b, i: (b, 0, 0))],
        out_specs=pl.BlockSpec((None, r, N_Q_HEADS * HEAD_DIM), lambda b, i: (b, i, 0)),
        out_shape=jax.ShapeDtypeStruct((bsz, s, N_Q_HEADS * HEAD_DIM), BF16),
        scratch_shapes=[pltpu.VMEM((N_Q_HEADS, r, 1), F32), pltpu.VMEM((N_Q_HEADS, r, 1), F32),
                        pltpu.VMEM((N_Q_HEADS // 2, r, LANES), F32)],
        compiler_params=_cparams(("parallel", "arbitrary"), 48),
        name="attn_dense",
    )(qb, kvb)


def _residual_ffn_norm(x_ref, y, g1_ref, gain_ref, sc_ref, sh_ref, xo_ref, f_ref):
    x_new = x_ref[...] + g1_ref[...] * y
    xo_ref[...] = x_new
    f_ref[...] = _rms_rows(x_new, gain_ref[...]) * (1.0 + sc_ref[...]) + sh_ref[...]


def _attn_out_kernel(x_ref, oa_ref, ob_ref, wa_ref, wb_ref, g1_ref, gain_ref, sc_ref, sh_ref, xo_ref, f_ref):
    y = _dot(oa_ref[...], wa_ref[...]) + _dot(ob_ref[...], wb_ref[...])
    _residual_ffn_norm(x_ref, y, g1_ref, gain_ref, sc_ref, sh_ref, xo_ref, f_ref)


def _attn_out(x_all, oa, ob, wa, wb, g1, gain, sc, sh):
    bsz, s, d = x_all.shape
    r = ROW_BLOCK
    row = lambda w: pl.BlockSpec((None, r, w), lambda b, i: (b, i, 0))
    return pl.pallas_call(
        _attn_out_kernel,
        grid=(bsz, s // r),
        in_specs=[row(d), row(oa.shape[2]), row(ob.shape[2]), _const_spec(wa.shape), _const_spec(wb.shape),
                  _seg_spec(d), _const_spec((1, d)), _seg_spec(d), _seg_spec(d)],
        out_specs=[row(d), row(d)],
        out_shape=[jax.ShapeDtypeStruct((bsz, s, d), F32)] * 2,
        compiler_params=_cparams(("parallel", "arbitrary"), 40),
        name="attn_out",
    )(x_all, oa, ob, wa, wb, g1, gain, sc, sh)


def _gla_out_kernel(x_ref, of_ref, ob_ref, r_ref, hg_ref, w_ref, g1_ref, gain_ref, sc_ref, sh_ref, xo_ref, f_ref):
    o = of_ref[...] + ob_ref[...]
    rr = r_ref[...]
    dv = o.shape[1] // GLA_HEADS
    y = None
    for h in range(GLA_HEADS):
        sl = slice(h * dv, (h + 1) * dv)
        rh = rr[:, sl]
        t = _rms_rows(o[:, sl], hg_ref[...]) * (rh * (1.0 / (1.0 + jnp.exp(-rh))))
        part = _dot(t.astype(BF16), w_ref[sl, :])
        y = part if y is None else y + part
    _residual_ffn_norm(x_ref, y, g1_ref, gain_ref, sc_ref, sh_ref, xo_ref, f_ref)


def _gla_out(x_all, o_f, o_b, r_all, head_gain, w_out, g1, gain, sc, sh, n_ctx):
    bsz, s, d = x_all.shape
    r = ROW_BLOCK
    skip = n_ctx // r
    row_in = lambda w: pl.BlockSpec((None, r, w), lambda b, i: (b, i + skip, 0))
    row_out = pl.BlockSpec((None, r, d), lambda b, i: (b, i, 0))
    return pl.pallas_call(
        _gla_out_kernel,
        grid=(bsz, (s - n_ctx) // r),
        in_specs=[row_in(d), row_in(d), row_in(d), row_in(d), _const_spec(head_gain.shape), _const_spec(w_out.shape),
                  _lat_spec(d), _const_spec((1, d)), _lat_spec(d), _lat_spec(d)],
        out_specs=[row_out, row_out],
        out_shape=[jax.ShapeDtypeStruct((bsz, s - n_ctx, d), F32)] * 2,
        compiler_params=_cparams(("parallel", "arbitrary"), 40),
        name="gla_out",
    )(x_all, o_f, o_b, r_all, head_gain, w_out, g1, gain, sc, sh)


def _peer_route_kernel(f_ref, wq_ref, keys_ref, gates_ref, experts_ref, q_ref, s_top_ref, i_top_ref):
    r = f_ref.shape[0]
    k16 = PEER_TOPK
    q_ref[...] = _dot(f_ref[...].astype(BF16), wq_ref[...]).astype(BF16)
    key_id = lax.broadcasted_iota(I32, (PEER_N_KEYS, r), 0).astype(F32)
    minus_inf = jnp.float32(-jnp.inf)

    def sub_key_topk(hp, carry):
        qs = q_ref[:, pl.ds(pl.multiple_of(hp * PEER_N_KEYS, PEER_N_KEYS), PEER_N_KEYS)]
        scores = _dot_nt(keys_ref[hp], qs)

        def take(k, v):
            m = jnp.max(v, axis=0, keepdims=True)
            idx = jnp.min(jnp.where(v == m, key_id, float(PEER_N_KEYS)), axis=0, keepdims=True)
            s_top_ref[hp, pl.ds(k, 1), :] = m
            i_top_ref[hp, pl.ds(k, 1), :] = idx
            return jnp.where(key_id == idx, minus_inf, v)

        lax.fori_loop(0, k16, take, scores)
        return carry

    lax.fori_loop(0, 2 * PEER_HEADS, sub_key_topk, 0)

    row8 = lax.broadcasted_iota(I32, (SUBLANES, r), 0)
    n_blocks = k16 + 1
    pos = jnp.concatenate([(row8 + (a_blk * SUBLANES if a_blk < 2 else (a_blk - 1) * k16)).astype(F32)
                           for a_blk in range(n_blocks)], axis=0)

    def product_topk(h, carry):
        s1, s2 = s_top_ref[2 * h], s_top_ref[2 * h + 1]
        i1, i2 = i_top_ref[2 * h], i_top_ref[2 * h + 1]
        cand, ids = [], []
        for blk in range(n_blocks):
            a = 0 if blk < 2 else blk - 1
            lo = SUBLANES if blk == 1 else 0
            c = s1[a:a + 1, :] + s2[lo:lo + SUBLANES, :]
            if a > 0:
                c = jnp.where(row8 < k16 // (a + 1), c, minus_inf)
            cand.append(c)
            ids.append(i1[a:a + 1, :] * PEER_N_KEYS + i2[lo:lo + SUBLANES, :])
        cand = jnp.concatenate(cand, axis=0)
        ids = jnp.concatenate(ids, axis=0)
        best = []
        for k in range(k16):
            m = jnp.max(cand, axis=0, keepdims=True)
            sel = jnp.min(jnp.where(cand == m, pos, float(k16 * k16)), axis=0, keepdims=True)
            hit = pos == sel
            expert = jnp.max(jnp.where(hit, ids, -1.0), axis=0, keepdims=True)
            experts_ref[pl.ds(h * k16 + k, 1), :] = expert.astype(I32)
            cand = jnp.where(hit, minus_inf, cand)
            best.append(m)
        e = [jnp.exp(b - best[0]) for b in best]
        den = functools.reduce(lambda x, y: x + y, e)
        for k in range(k16):
            gates_ref[pl.ds(h * k16 + k, 1), :] = e[k] / den
        return carry

    lax.fori_loop(0, PEER_HEADS, product_topk, 0)


def _peer_route(f_flat, wq_bf16, keys_bf16):
    t, d = f_flat.shape
    r = ROW_BLOCK
    nq = wq_bf16.shape[1]
    out = pl.BlockSpec((PEER_PAIRS, r), lambda i: (0, i))
    return pl.pallas_call(
        _peer_route_kernel,
        grid=(t // r,),
        in_specs=[pl.BlockSpec((r, d), lambda i: (i, 0)), pl.BlockSpec((d, nq), lambda i: (0, 0)),
                  pl.BlockSpec(keys_bf16.shape, lambda i: (0, 0, 0))],
        out_specs=[out, out],
        out_shape=[jax.ShapeDtypeStruct((PEER_PAIRS, t), F32), jax.ShapeDtypeStruct((PEER_PAIRS, t), I32)],
        scratch_shapes=[pltpu.VMEM((r, nq), BF16),
                        pltpu.VMEM((2 * PEER_HEADS, PEER_TOPK, r), F32),
                        pltpu.VMEM((2 * PEER_HEADS, PEER_TOPK, r), F32)],
        compiler_params=_cparams(("parallel",), 40),
        name="peer_route",
    )(f_flat, wq_bf16, keys_bf16)


def _pack_table(tbl):
    e, d = tbl.shape
    bits = lax.bitcast_convert_type(tbl.astype(BF16), jnp.uint16).astype(jnp.uint32).reshape(e // 2, 2, d)
    words = (bits[:, 0, :] << 16) | bits[:, 1, :]
    return lax.bitcast_convert_type(words, I32).reshape(e // 2 * (d // LANES), LANES)


def _expert_row(tbl_ref, row8, shift):
    w = tbl_ref[pl.ds(pl.multiple_of(row8, SUBLANES), SUBLANES), :]
    return pltpu.bitcast(lax.shift_left(w, jnp.full(w.shape, shift, I32)) & jnp.int32(-65536), F32)


def _fold8(vs, row):
    vs = [vs[n] for n in (0, 4, 2, 6, 1, 5, 3, 7)]
    step = SUBLANES // 2
    while len(vs) > 1:
        keep = (row % (2 * step)) < step
        nxt = []
        for a, b in zip(vs[0::2], vs[1::2]):
            if 2 * step == SUBLANES:
                nxt.append(jnp.where(keep, a, b) + pltpu.roll(jnp.where(keep, b, a), step, 0))
            else:
                nxt.append(jnp.where(keep, a + pltpu.roll(a, SUBLANES - step, 0), b + pltpu.roll(b, step, 0)))
        vs, step = nxt, step // 2
    return vs[0]


def _index_stream(hbm_refs, smem_refs, sem_ref, i, n_steps, width):
    slot = i % 2

    def copies(step, slot_):
        return [pltpu.make_async_copy(h.at[pl.ds(pl.multiple_of(step * width, width), width)],
                                      s.at[pl.ds(pl.multiple_of(slot_ * width, width), width)],
                                      sem_ref.at[n, slot_])
                for n, (h, s) in enumerate(zip(hbm_refs, smem_refs))]

    @pl.when(i == 0)
    def _():
        for c in copies(0, 0):
            c.start()

    for c in copies(i, slot):
        c.wait()

    @pl.when(i + 1 < n_steps)
    def _():
        for c in copies(i + 1, 1 - slot):
            c.start()

    return slot * width


def _load_table_once(tbl_hbm, tbl_ref, sem_ref, i):
    @pl.when(i == 0)
    def _():
        c = pltpu.make_async_copy(tbl_hbm, tbl_ref, sem_ref.at[0])
        c.start()
        c.wait()


def _erf(x):
    x = jnp.clip(x, -4.0, 4.0)
    z = x * x
    p = -2.72614225801306e-10
    for c in (2.77068142495902e-08, -2.10102402082508e-06, -5.69250639462346e-05, -7.34990630326855e-04,
              -2.95459980854025e-03, -1.60960333262415e-02):
        p = p * z + c
    q = -1.45660718464996e-05
    for c in (-2.13374055278905e-04, -1.68282697438203e-03, -7.37332916720468e-03, -1.42647390514189e-02):
        q = q * z + c
    return x * p / q


def _peer_up_kernel(row_hbm, shift_hbm, tbl_hbm, x_ref, gate_ref, o_ref,
                    row_smem, shift_smem, tbl_ref, part_ref, isem, tsem):
    i = pl.program_id(0)
    tb = x_ref.shape[0]
    width = tb * PEER_PAIRS
    _load_table_once(tbl_hbm, tbl_ref, tsem, i)
    base = _index_stream((row_hbm, shift_hbm), (row_smem, shift_smem), isem, i, pl.num_programs(0), width)
    row = lax.broadcasted_iota(I32, (SUBLANES, LANES), 0)

    def token(t, carry):
        xt = x_ref[t]
        off = base + t * PEER_PAIRS
        for g in range(PEER_PAIRS // SUBLANES):
            prods = []
            for s in range(SUBLANES):
                n = off + g * SUBLANES + s
                prods.append(_expert_row(tbl_ref, row_smem[n], shift_smem[n]) * xt)
            part_ref[pl.ds(pl.multiple_of(t * PEER_PAIRS + g * SUBLANES, SUBLANES), SUBLANES), :] = _fold8(prods, row)
        return carry

    lax.fori_loop(0, tb, token, 0)

    ones = jnp.ones((SUBLANES, LANES), BF16)

    def lane_sums(g, carry):
        a = part_ref[pl.ds(pl.multiple_of(g * SUBLANES * PEER_PAIRS, SUBLANES * PEER_PAIRS), SUBLANES * PEER_PAIRS), :]
        hi = a.astype(BF16)
        lo = (a - hi.astype(F32)).astype(BF16)
        sums = _dot_nt(ones, hi) + _dot_nt(ones, lo)
        for k in range(SUBLANES):
            o_ref[pl.ds(g * SUBLANES + k, 1), :] = sums[0:1, k * PEER_PAIRS:(k + 1) * PEER_PAIRS]
        return carry

    lax.fori_loop(0, tb // SUBLANES, lane_sums, 0)
    act = o_ref[...]
    o_ref[...] = gate_ref[...] * (0.5 * act * (1.0 + _erf(act * (2.0 ** -0.5))))


def _peer_up(rows, shifts, tbl_packed, f_tiles, gates):
    t = f_tiles.shape[0]
    tb = PEER_BLOCK
    width = tb * PEER_PAIRS
    return pl.pallas_call(
        _peer_up_kernel,
        grid=(t // tb,),
        in_specs=[pl.BlockSpec(memory_space=pl.ANY), pl.BlockSpec(memory_space=pl.ANY),
                  pl.BlockSpec(memory_space=pl.ANY),
                  pl.BlockSpec((tb, SUBLANES, LANES), lambda i: (i, 0, 0)),
                  pl.BlockSpec((tb, PEER_PAIRS), lambda i: (i, 0))],
        out_specs=pl.BlockSpec((tb, PEER_PAIRS), lambda i: (i, 0)),
        out_shape=jax.ShapeDtypeStruct((t, PEER_PAIRS), F32),
        scratch_shapes=[pltpu.SMEM((2 * width,), I32), pltpu.SMEM((2 * width,), I32),
                        pltpu.VMEM(tbl_packed.shape, I32),
                        pltpu.VMEM((tb * PEER_PAIRS, LANES), F32),
                        pltpu.SemaphoreType.DMA((2, 2)), pltpu.SemaphoreType.DMA((1,))],
        compiler_params=_cparams(("arbitrary",), 52),
        name="peer_up",
    )(rows, shifts, tbl_packed, f_tiles, gates)


def _peer_down_kernel(row_hbm, coef_hbm, tbl_hbm, x_ref, g2_ref, o_ref,
                      row_smem, coef_smem, tbl_ref, isem, tsem):
    i = pl.program_id(0)
    tb = x_ref.shape[0]
    width = tb * PEER_PAIRS
    _load_table_once(tbl_hbm, tbl_ref, tsem, i)
    base = _index_stream((row_hbm, coef_hbm), (row_smem, coef_smem), isem, i, pl.num_programs(0), width)
    g2 = g2_ref[...]
    n_acc = 4

    high = jnp.int32(-65536)

    def token(t, carry):
        off = base + t * PEER_PAIRS
        accs = [jnp.zeros((SUBLANES, LANES), F32)] * n_acc
        for p in range(PEER_PAIRS):
            n = off + p
            w = tbl_ref[pl.ds(pl.multiple_of(row_smem[n], SUBLANES), SUBLANES), :]
            cw = jnp.full((SUBLANES, LANES), coef_smem[n], I32)
            val = pltpu.bitcast(lax.shift_left(w, cw & 0xFFFF) & high, F32)
            accs[p % n_acc] = accs[p % n_acc] + val * pltpu.bitcast(cw & high, F32)
        o_ref[t] = x_ref[t] + g2 * ((accs[0] + accs[1]) + (accs[2] + accs[3]))
        return carry

    lax.fori_loop(0, tb, token, 0)


def _peer_down(rows, coef_words, tbl_packed, x_tiles, g2_tiles, g2_index):
    t = x_tiles.shape[0]
    tb = PEER_BLOCK
    width = tb * PEER_PAIRS
    tok = pl.BlockSpec((tb, SUBLANES, LANES), lambda i: (i, 0, 0))
    return pl.pallas_call(
        _peer_down_kernel,
        grid=(t // tb,),
        in_specs=[pl.BlockSpec(memory_space=pl.ANY)] * 3
        + [tok, pl.BlockSpec((None, SUBLANES, LANES), lambda i: (g2_index(i), 0, 0))],
        out_specs=tok,
        out_shape=jax.ShapeDtypeStruct(x_tiles.shape, F32),
        scratch_shapes=[pltpu.SMEM((2 * width,), I32), pltpu.SMEM((2 * width,), I32),
                        pltpu.VMEM(tbl_packed.shape, I32),
                        pltpu.SemaphoreType.DMA((2, 2)), pltpu.SemaphoreType.DMA((1,))],
        compiler_params=_cparams(("arbitrary",), 52),
        name="peer_down",
    )(rows, coef_words, tbl_packed, x_tiles, g2_tiles)


def _peer_ffn(f, x_res, g2_tiles, g2_index, wq_bf16, keys_bf16, u_packed, v_packed):
    t, d = f.shape
    gates_t, experts_t = _peer_route(f, wq_bf16, keys_bf16)
    experts = experts_t.T.reshape(-1)
    rows = (experts >> 1) * SUBLANES
    shifts = (experts & 1) * 16
    act = _peer_up(rows, shifts, u_packed, f.reshape(t, SUBLANES, LANES), gates_t.T)
    coef_bits = lax.bitcast_convert_type(act.reshape(-1).astype(BF16), jnp.uint16).astype(I32)
    out = _peer_down(rows, (coef_bits << 16) | shifts, v_packed, x_res.reshape(t, SUBLANES, LANES), g2_tiles, g2_index)
    return out.reshape(t, d)


def _gla_in_kernel(x_ref, gain_ref, sc_ref, sh_ref, w_ref, wg_ref, gw_ref, gb_ref,
                   q_ref, k_ref, v_ref, r_ref, lgf_ref, lgb_ref):
    h = _modulated_norm(x_ref, gain_ref, sc_ref, sh_ref).astype(BF16)
    kd = q_ref.shape[1]
    vd = v_ref.shape[1]
    dk = kd // GLA_HEADS
    q_ref[...] = _dot(h, w_ref[:, 0:kd]) * dk ** -0.5
    k_ref[...] = _dot(h, w_ref[:, kd:2 * kd])
    v_ref[...] = _dot(h, w_ref[:, 2 * kd:2 * kd + vd]).astype(BF16)
    r_ref[...] = _dot(h, w_ref[:, 2 * kd + vd:2 * kd + 2 * vd])
    low_rank = _dot(h, wg_ref[...]).astype(BF16)
    z = _dot(low_rank, gw_ref[...]) + gb_ref[...]
    log_sig = jnp.minimum(z, 0.0) - jnp.log(1.0 + jnp.exp(-jnp.abs(z)))
    lgf_ref[...] = log_sig[:, 0:kd] / GLA_GATE_NORM
    lgb_ref[...] = log_sig[:, kd:2 * kd] / GLA_GATE_NORM


def _gla_in(x_all, gain, sc, sh, w_main, w_gate, gate_w, gate_b, kd, vd):
    bsz, s, d = x_all.shape
    r = ROW_BLOCK
    row = lambda w: pl.BlockSpec((None, r, w), lambda b, i: (b, i, 0))
    shp = lambda w, dt: jax.ShapeDtypeStruct((bsz, s, w), dt)
    return pl.pallas_call(
        _gla_in_kernel,
        grid=(bsz, s // r),
        in_specs=[row(d), _const_spec((1, d)), _seg_spec(d), _seg_spec(d), _const_spec(w_main.shape),
                  _const_spec(w_gate.shape), _const_spec(gate_w.shape), _const_spec(gate_b.shape)],
        out_specs=[row(kd), row(kd), row(vd), row(vd), row(kd), row(kd)],
        out_shape=[shp(kd, F32), shp(kd, F32), shp(vd, BF16), shp(vd, F32), shp(kd, F32), shp(kd, F32)],
        compiler_params=_cparams(("parallel", "arbitrary"), 48),
        name="gla_in",
    )(x_all, gain, sc, sh, w_main, w_gate, gate_w, gate_b)


def _gla_scan_kernel(qf_ref, kf_ref, vf_ref, gf_ref, qb_ref, kb_ref, vb_ref, gb_ref, of_ref, ob_ref,
                     sf_ref, sb_ref):
    i = pl.program_id(1)
    c = GLA_CHUNK
    dk = qf_ref.shape[1] // GLA_HEADS
    dv = vf_ref.shape[1] // GLA_HEADS
    n_chunks = qf_ref.shape[0] // c

    @pl.when(i == 0)
    def _():
        sf_ref[...] = jnp.zeros(sf_ref.shape, F32)
        sb_ref[...] = jnp.zeros(sb_ref.shape, F32)

    t_row = lax.broadcasted_iota(I32, (c, c), 0)
    t_col = lax.broadcasted_iota(I32, (c, c), 1)
    tri = {False: t_col <= t_row, True: t_col >= t_row}
    tri_bf16 = {rev: jnp.where(m, 1.0, 0.0).astype(BF16) for rev, m in tri.items()}

    def chunk(q_ref, k_ref, v_ref, g_ref, o_ref, s_ref, n, rev):
        rows = slice(n * c, (n + 1) * c)
        for h in range(GLA_HEADS):
            kl = slice(h * dk, (h + 1) * dk)
            vl = slice(h * dv, (h + 1) * dv)
            g = g_ref[rows, kl]
            b = _split_dot_left(tri_bf16[rev], g)
            total = jnp.sum(g, axis=0, keepdims=True)
            q_dec = (q_ref[rows, kl] * jnp.exp(b)).astype(BF16)
            kk = k_ref[rows, kl]
            k_end = kk * jnp.exp(total - b)
            k_neg = (kk * jnp.exp(-b)).astype(BF16)
            vv = v_ref[rows, vl]
            state = s_ref[h]
            inter = _dot(q_dec, state.astype(BF16))
            att = jnp.where(tri[rev], _dot_nt(q_dec, k_neg), 0.0)
            o_ref[rows, vl] = inter + _dot(att.astype(BF16), vv)
            decay_col = jnp.exp(jnp.sum(g.T, axis=1, keepdims=True))
            s_ref[h] = decay_col * state + _dot(k_end.T.astype(BF16), vv)

    for n in range(n_chunks):
        chunk(qf_ref, kf_ref, vf_ref, gf_ref, of_ref, sf_ref, n, False)
        chunk(qb_ref, kb_ref, vb_ref, gb_ref, ob_ref, sb_ref, n_chunks - 1 - n, True)


def _split_dot_left(a_bf16, b):
    hi = b.astype(BF16)
    lo = (b - hi.astype(F32)).astype(BF16)
    return _dot(a_bf16, hi) + _dot(a_bf16, lo)


def _gla_scan(q, k, v, lgf, lgb):
    bsz, s, kd = q.shape
    vd = v.shape[2]
    r = ROW_BLOCK
    n = s // r
    fwd = lambda w: pl.BlockSpec((None, r, w), lambda b, i: (b, i, 0))
    bwd = lambda w: pl.BlockSpec((None, r, w), lambda b, i: (b, jnp.where(i == 0, 0, n - i), 0))
    return pl.pallas_call(
        _gla_scan_kernel,
        grid=(bsz, n),
        in_specs=[fwd(kd), fwd(kd), fwd(vd), fwd(kd), bwd(kd), bwd(kd), bwd(vd), bwd(kd)],
        out_specs=[fwd(vd), bwd(vd)],
        out_shape=[jax.ShapeDtypeStruct((bsz, s, vd), F32)] * 2,
        scratch_shapes=[pltpu.VMEM((GLA_HEADS, kd // GLA_HEADS, vd // GLA_HEADS), F32)] * 2,
        compiler_params=_cparams(("parallel", "arbitrary"), 40),
        name="gla_scan",
    )(q, k, v, lgf, q, k, v, lgb)


def _final_norm_kernel(x_ref, gain_ref, o_ref):
    o_ref[...] = _rms_rows(x_ref[...], gain_ref[...])


def _final_norm(x, gain):
    t, d = x.shape
    r = 2 * ROW_BLOCK
    return pl.pallas_call(
        _final_norm_kernel,
        grid=(t // r,),
        in_specs=[pl.BlockSpec((r, d), lambda i: (i, 0)), pl.BlockSpec((1, d), lambda i: (0, 0))],
        out_specs=pl.BlockSpec((r, d), lambda i: (i, 0)),
        out_shape=jax.ShapeDtypeStruct((t, d), F32),
        compiler_params=_cparams(("parallel",), 40),
        name="final_norm",
    )(x, gain)


def _rope_tables(n_ctx, length):
    rows = length // GRID_W
    row = jnp.repeat(jnp.arange(rows), GRID_W)
    col = jnp.tile(jnp.arange(GRID_W), rows)
    pos = jnp.stack([row, col], axis=-1).astype(F32)
    inv = ROPE_THETA ** (-jnp.arange(ROPE_PAIRS, dtype=F32) / ROPE_PAIRS)
    ang = pos[:, :, None] * inv
    cos, sin = jnp.cos(ang), jnp.sin(ang)
    cos64 = jnp.concatenate([cos[:, 0], cos[:, 0], cos[:, 1], cos[:, 1]], axis=-1)
    sin64 = jnp.concatenate([-sin[:, 0], sin[:, 0], -sin[:, 1], sin[:, 1]], axis=-1)
    cos_t = jnp.concatenate([jnp.ones((n_ctx, LANES), F32), jnp.tile(cos64, (1, 2))], axis=0)
    sin_t = jnp.concatenate([jnp.zeros((n_ctx, LANES), F32), jnp.tile(sin64, (1, 2))], axis=0)
    return cos_t, sin_t


def kernel(x, c, ctx, c_ctx, ada_w, ada_b, norm_mix, norm_ffn, attn_w_in, attn_sink, attn_q_gain, attn_k_gain,
           attn_w_out, gla_w_in, gla_gate_w, gla_gate_b, gla_head_gain, gla_w_out, peer_wq, peer_keys, peer_u,
           peer_v, final_norm):
    bsz, length, d = x.shape
    n_ctx = ctx.shape[1]
    s = n_ctx + length
    assert n_ctx == ROW_BLOCK and d == SUBLANES * LANES and length % ROW_BLOCK == 0

    cond8 = jnp.zeros((SUBLANES, d), F32).at[:bsz].set(c).at[bsz].set(c_ctx)
    mod = _adaln(cond8, ada_w, ada_b)

    def seg_vectors(layer):
        lat = mod[layer, :bsz].reshape(bsz, 6, d)
        cx = jnp.broadcast_to(mod[layer, bsz].reshape(1, 6, d), (bsz, 6, d))
        both = jnp.stack([cx, lat], axis=1)
        return [both[:, :, n, :].reshape(bsz, 2, 1, d) for n in range(6)]

    def gate_tiles(g):
        return g.reshape(bsz * 2, SUBLANES, LANES)

    x_all = jnp.concatenate([ctx, x], axis=1)
    row = lambda v: v.reshape(1, -1)

    sh1, sc1, g1, sh2, sc2, g2 = seg_vectors(0)
    cos_t, sin_t = _rope_tables(n_ctx, length)
    lane = jnp.arange(LANES)
    gmean = jnp.where((lane[:, None] // HEAD_DIM) == (lane[None, :] // HEAD_DIM), 1.0 / HEAD_DIM, 0.0).astype(BF16)
    qa, qb, kva, kvb = _attn_in(x_all, row(norm_mix[0]), sc1, sh1, attn_w_in[0].astype(BF16), cos_t, sin_t, gmean,
                                row(jnp.tile(attn_q_gain[0], 2)), row(jnp.tile(attn_k_gain[0], 2)))
    oa = _attn_win(attn_sink[0], qa, kva, n_ctx)
    ob = _attn_dense(qb, kvb, n_ctx)
    w_out = attn_w_out[0].astype(BF16)
    half = w_out.shape[0] // 2
    x_all, f_all = _attn_out(x_all, oa, ob, w_out[:half], w_out[half:], g1, row(norm_ffn[0]), sc2, sh2)
    blocks_per_batch = s // PEER_BLOCK
    ctx_blocks = n_ctx // PEER_BLOCK
    x_all = _peer_ffn(
        f_all.reshape(bsz * s, d), x_all.reshape(bsz * s, d), gate_tiles(g2),
        lambda i: 2 * (i // blocks_per_batch) + jnp.where(i % blocks_per_batch < ctx_blocks, 0, 1),
        peer_wq[0].astype(BF16), peer_keys[0].reshape(2 * PEER_HEADS, PEER_N_KEYS, -1).astype(BF16),
        _pack_table(peer_u[0]), _pack_table(peer_v[0])).reshape(bsz, s, d)

    sh1, sc1, g1, sh2, sc2, g2 = seg_vectors(1)
    w_in = gla_w_in[0]
    kd = gla_gate_w.shape[-1]
    vd = (w_in.shape[1] - 2 * kd - 2 * gla_gate_w.shape[-2]) // 2
    rank = gla_gate_w.shape[-2]
    n_main = 2 * kd + 2 * vd
    w_gate = jnp.zeros((d, LANES), F32).at[:, :2 * rank].set(w_in[:, n_main:]).astype(BF16)
    gate_w = (jnp.zeros((LANES, 2 * kd), F32).at[:rank, :kd].set(gla_gate_w[0, 0])
              .at[rank:2 * rank, kd:].set(gla_gate_w[0, 1])).astype(BF16)
    q, k, v, r_all, lgf, lgb = _gla_in(x_all, row(norm_mix[1]), sc1, sh1, w_in[:, :n_main].astype(BF16), w_gate,
                                       gate_w, gla_gate_b[0].reshape(1, 2 * kd), kd, vd)
    o_f, o_b = _gla_scan(q, k, v, lgf, lgb)
    x_lat, f_lat = _gla_out(x_all, o_f, o_b, r_all, row(gla_head_gain[0]), gla_w_out[0].astype(BF16),
                            g1, row(norm_ffn[1]), sc2, sh2, n_ctx)
    lat_blocks = length // PEER_BLOCK
    x_lat = _peer_ffn(
        f_lat.reshape(bsz * length, d), x_lat.reshape(bsz * length, d), gate_tiles(g2),
        lambda i: 2 * (i // lat_blocks) + 1,
        peer_wq[1].astype(BF16), peer_keys[1].reshape(2 * PEER_HEADS, PEER_N_KEYS, -1).astype(BF16),
        _pack_table(peer_u[1]), _pack_table(peer_v[1]))
    return _final_norm(x_lat, row(final_norm)).reshape(bsz, length, d)
```

```python
import functools
import math

import jax
import jax.numpy as jnp
from jax import lax
from jax.experimental import pallas as pl
from jax.experimental.pallas import tpu as pltpu

F32 = jnp.float32
BF16 = jnp.bfloat16
I32 = jnp.int32

SUBLANES = 8
LANES = 128
VMEM_BYTES_V7X = 64 * 1024 * 1024

EPS = 1e-6
NEG_INF = -1e30
GRID_W = 64
WINDOW = 128
HEAD_DIM = 64
ROPE_THETA = 10000.0
ROPE_PAIRS = HEAD_DIM // 4
N_Q_HEADS = 8
GLA_HEADS = 4
GLA_GATE_NORM = 16.0
GLA_CHUNK = 64
PEER_HEADS = 8
PEER_N_KEYS = 128
PEER_TOPK = 16
PEER_PAIRS = PEER_HEADS * PEER_TOPK
ROW_BLOCK = 256
PEER_BLOCK = 128
KV_CHUNK = 768

NT_DIMS = (((1,), (1,)), ((), ()))


def _cparams(sem, vmem_mb):
    return pltpu.CompilerParams(dimension_semantics=sem, vmem_limit_bytes=vmem_mb * 1024 * 1024)


def _dot(a, b):
    return jnp.dot(a, b, preferred_element_type=F32)


def _dot_nt(a, b):
    return lax.dot_general(a, b, NT_DIMS, preferred_element_type=F32)


def _split_dot(a, b_bf16):
    hi = a.astype(BF16)
    lo = (a - hi.astype(F32)).astype(BF16)
    return _dot(hi, b_bf16) + _dot(lo, b_bf16)


def _rms_rows(x, gain):
    ms = jnp.mean(x * x, axis=-1, keepdims=True)
    return x * lax.rsqrt(ms + EPS) * gain


def _adaln_kernel(cond_ref, w_ref, b_ref, o_ref):
    cnd = cond_ref[...]
    act = cnd * (1.0 / (1.0 + jnp.exp(-cnd)))
    o_ref[...] = jnp.dot(act, w_ref[...], precision=lax.Precision.HIGHEST,
                         preferred_element_type=F32) + b_ref[...]


def _adaln(cond8, ada_w, ada_b):
    depth, d, n6 = ada_w.shape
    tn = n6 // 4
    return pl.pallas_call(
        _adaln_kernel,
        grid=(depth, 4),
        in_specs=[pl.BlockSpec((SUBLANES, d), lambda l, j: (0, 0)),
                  pl.BlockSpec((None, d, tn), lambda l, j: (l, 0, j)),
                  pl.BlockSpec((None, 1, tn), lambda l, j: (l, 0, j))],
        out_specs=pl.BlockSpec((None, SUBLANES, tn), lambda l, j: (l, 0, j)),
        out_shape=jax.ShapeDtypeStruct((depth, SUBLANES, n6), F32),
        compiler_params=_cparams(("arbitrary", "arbitrary"), 40),
        name="adaln",
    )(cond8, ada_w, ada_b.reshape(depth, 1, n6))


def _modulated_norm(x_ref, gain_ref, sc_ref, sh_ref):
    x = x_ref[...]
    return _rms_rows(x, gain_ref[...]) * (1.0 + sc_ref[...]) + sh_ref[...]


def _seg_spec(d):
    return pl.BlockSpec((None, None, 1, d), lambda b, i: (b, jnp.minimum(i, 1), 0, 0))


def _lat_spec(d):
    return pl.BlockSpec((None, None, 1, d), lambda b, i: (b, 1, 0, 0))


def _const_spec(shape):
    nd = len(shape)
    return pl.BlockSpec(shape, lambda b, i: (0,) * nd)


def _swap16(t, lane):
    return jnp.where((lane % 32) < 16, pltpu.roll(t, LANES - 16, 1), pltpu.roll(t, 16, 1))


def _attn_in_kernel(x_ref, gain_ref, sc_ref, sh_ref, w_ref, cos_ref, sin_ref, gmean_ref, qg_ref, kg_ref,
                    qa_ref, kva_ref, qbt_ref, kb_ref, vbt_ref):
    h = _modulated_norm(x_ref, gain_ref, sc_ref, sh_ref).astype(BF16)
    p = _dot(h, w_ref[...])
    lane = lax.broadcasted_iota(I32, (1, LANES), 1)
    low = lane < HEAD_DIM
    cs, sn = cos_ref[...], sin_ref[...]
    scale = HEAD_DIM ** -0.5

    def rope(t):
        return t * cs + _swap16(t, lane) * sn

    def head_norm(t, g_ref):
        ms = _split_dot(t * t, gmean_ref[...])
        return t * lax.rsqrt(ms + EPS) * g_ref[...]

    def tile(j):
        return p[:, j * LANES:(j + 1) * LANES]

    def padded_heads(j, t):
        tr = pltpu.roll(t, HEAD_DIM, 1)
        zero = jnp.zeros_like(t)
        if j // 2 == 0:
            return jnp.where(low, t, zero), jnp.where(low, tr, zero)
        return jnp.where(low, zero, tr), jnp.where(low, zero, t)

    for j in range(4):
        for par, t in enumerate(padded_heads(j, rope(tile(j)) * scale)):
            qa_ref[:, (2 * j + par) * LANES:(2 * j + par + 1) * LANES] = t.astype(BF16)
        for par, t in enumerate(padded_heads(j, rope(head_norm(tile(6 + j), qg_ref)) * scale)):
            qbt_ref[(2 * j + par) * LANES:(2 * j + par + 1) * LANES, :] = t.T.astype(BF16)
    va = tile(5)
    kva_ref[:, 0:LANES] = rope(tile(4)).astype(BF16)
    kva_ref[:, LANES:2 * LANES] = va.astype(BF16)
    kva_ref[:, 2 * LANES:3 * LANES] = pltpu.roll(va, HEAD_DIM, 1).astype(BF16)
    vb = tile(11)
    kb_ref[...] = rope(head_norm(tile(10), kg_ref)).astype(BF16)
    vbt_ref[0:LANES, :] = vb.T.astype(BF16)
    vbt_ref[LANES:2 * LANES, :] = pltpu.roll(vb, HEAD_DIM, 1).T.astype(BF16)


def _attn_in(x_all, gain, sc, sh, w_bf16, cos_t, sin_t, gmean, qg, kg):
    bsz, s, d = x_all.shape
    n = w_bf16.shape[1]
    r = ROW_BLOCK
    row = lambda w: pl.BlockSpec((None, r, w), lambda b, i: (b, i, 0))
    col = lambda w: pl.BlockSpec((None, w, r), lambda b, i: (b, 0, i))
    tab = pl.BlockSpec((r, LANES), lambda b, i: (i, 0))
    return pl.pallas_call(
        _attn_in_kernel,
        grid=(bsz, s // r),
        in_specs=[row(d), _const_spec((1, d)), _seg_spec(d), _seg_spec(d), _const_spec((d, n)), tab, tab,
                  _const_spec((LANES, LANES)), _const_spec((1, LANES)), _const_spec((1, LANES))],
        out_specs=[row(N_Q_HEADS * LANES), row(3 * LANES), col(N_Q_HEADS * LANES), row(LANES), col(2 * LANES)],
        out_shape=[jax.ShapeDtypeStruct((bsz, s, N_Q_HEADS * LANES), BF16),
                   jax.ShapeDtypeStruct((bsz, s, 3 * LANES), BF16),
                   jax.ShapeDtypeStruct((bsz, N_Q_HEADS * LANES, s), BF16),
                   jax.ShapeDtypeStruct((bsz, s, LANES), BF16),
                   jax.ShapeDtypeStruct((bsz, 2 * LANES, s), BF16)],
        compiler_params=_cparams(("parallel", "arbitrary"), 40),
        name="attn_in",
    )(x_all, gain, sc, sh, w_bf16, cos_t, sin_t, gmean, qg, kg)


def _value_tile(kvh, par, v_nat, v_swap):
    return v_nat if kvh == par else v_swap


def _attn_win_kernel(sink_ref, q_ref, cur_ref, prev_ref, next_ref, ctx_ref, o_ref, *, n_ctx, n_tok):
    i = pl.program_id(1)
    r = ROW_BLOCK
    kv = jnp.concatenate([prev_ref[...], cur_ref[...], next_ref[...], ctx_ref[...]], axis=0)
    k, v_nat, v_swap = kv[:, 0:LANES], kv[:, LANES:2 * LANES], kv[:, 2 * LANES:3 * LANES]
    n_band = 2 * r
    qtok = i * r + lax.broadcasted_iota(I32, (r, 1), 0)
    col = lax.broadcasted_iota(I32, (1, n_band + n_ctx), 1)
    ktok = i * r - WINDOW + col
    valid = (col >= n_band) | ((ktok >= n_ctx) & (ktok < n_tok) & (qtok >= n_ctx)
                               & (jnp.abs(qtok - ktok) <= WINDOW))
    low = lax.broadcasted_iota(I32, (1, LANES), 1) < HEAD_DIM
    for j in range(N_Q_HEADS // 2):
        outs = []
        for par in range(2):
            hq = 2 * j + par
            s = _dot_nt(q_ref[:, hq * LANES:(hq + 1) * LANES], k)
            s = jnp.where(valid, s, NEG_INF)
            sink = sink_ref[hq]
            m = jnp.maximum(jnp.max(s, axis=1, keepdims=True), sink)
            e = jnp.exp(s - m)
            den = jnp.sum(e, axis=1, keepdims=True) + jnp.exp(sink - m)
            pv = _dot(e.astype(BF16), _value_tile(j // 2, par, v_nat, v_swap))
            outs.append(pv / den)
        o_ref[:, j * LANES:(j + 1) * LANES] = jnp.where(low, outs[0], outs[1]).astype(BF16)


def _attn_win(sink, qa, kva, n_ctx):
    bsz, s, _ = qa.shape
    r = ROW_BLOCK
    half = r // 2
    last_half = s // half - 1
    kw = 3 * LANES
    return pl.pallas_call(
        functools.partial(_attn_win_kernel, n_ctx=n_ctx, n_tok=s),
        grid=(bsz, s // r),
        in_specs=[pl.BlockSpec(memory_space=pltpu.SMEM),
                  pl.BlockSpec((None, r, N_Q_HEADS * LANES), lambda b, i: (b, i, 0)),
                  pl.BlockSpec((None, r, kw), lambda b, i: (b, i, 0)),
                  pl.BlockSpec((None, half, kw), lambda b, i: (b, jnp.maximum(2 * i - 1, 0), 0)),
                  pl.BlockSpec((None, half, kw), lambda b, i: (b, jnp.minimum(2 * i + 2, last_half), 0)),
                  pl.BlockSpec((None, n_ctx, kw), lambda b, i: (b, 0, 0))],
        out_specs=pl.BlockSpec((None, r, N_Q_HEADS * HEAD_DIM), lambda b, i: (b, i, 0)),
        out_shape=jax.ShapeDtypeStruct((bsz, s, N_Q_HEADS * HEAD_DIM), BF16),
        compiler_params=_cparams(("parallel", "arbitrary"), 40),
        name="attn_win",
    )(sink, qa, kva, kva, kva, kva)


def _attn_dense_kernel(qt_ref, k_ref, vt_ref, o_ref, *, n_ctx, n_tok):
    i = pl.program_id(1)
    r = qt_ref.shape[1]
    low = lax.broadcasted_iota(I32, (LANES, 1), 0) < HEAD_DIM

    def head_pair(j, n_chunks, size):
        def scores(c):
            k = k_ref[pl.ds(pl.multiple_of(c * size, size), size), :]
            return tuple(_dot(k, qt_ref[(2 * j + par) * LANES:(2 * j + par + 1) * LANES, :]) for par in range(2))

        def chunk(c, carry):
            stats, acc, sts = list(carry[:4]), carry[4], carry[5:]
            nxt = scores(jnp.minimum(c + 1, n_chunks - 1))
            start = pl.multiple_of(c * size, size)
            upd = []
            for par in range(2):
                m_old, l_old = stats[2 * par], stats[2 * par + 1]
                m_new = jnp.maximum(m_old, jnp.max(sts[par], axis=0, keepdims=True))
                alpha = jnp.exp(m_old - m_new)
                e = jnp.exp(sts[par] - m_new)
                stats[2 * par] = m_new
                stats[2 * par + 1] = alpha * l_old + jnp.sum(e, axis=0, keepdims=True)
                v0 = 0 if j // 2 == par else LANES
                pv = _dot(vt_ref[v0:v0 + LANES, pl.ds(start, size)], e.astype(BF16))
                upd.append(alpha * acc + pv)
            return (*stats, jnp.where(low, upd[0], upd[1]), *nxt)

        row = lambda v: jnp.full((1, r), v, F32)
        init = (row(NEG_INF), row(0.0), row(NEG_INF), row(0.0), jnp.zeros((LANES, r), F32), *scores(0))
        _, l_e, _, l_o, acc = lax.fori_loop(0, n_chunks, chunk, init)[:5]
        out_t = acc * jnp.where(low, 1.0 / l_e, 1.0 / l_o)
        o_ref[:, j * LANES:(j + 1) * LANES] = out_t.T.astype(BF16)

    @pl.when(i == 0)
    def _():
        for j in range(N_Q_HEADS // 2):
            head_pair(j, 1, n_ctx)

    @pl.when(i > 0)
    def _():
        for j in range(N_Q_HEADS // 2):
            head_pair(j, n_tok // KV_CHUNK, KV_CHUNK)


def _attn_dense(qbt, kb, vbt, n_ctx):
    bsz, s, _ = kb.shape
    r = ROW_BLOCK
    assert s % KV_CHUNK == 0 and n_ctx == r
    return pl.pallas_call(
        functools.partial(_attn_dense_kernel, n_ctx=n_ctx, n_tok=s),
        grid=(bsz, s // r),
        in_specs=[pl.BlockSpec((None, N_Q_HEADS * LANES, r), lambda b, i: (b, 0, i)),
                  pl.BlockSpec((None, s, LANES), lambda b, i: (b, 0, 0)),
                  pl.BlockSpec((None, 2 * LANES, s), lambda b, i: (b, 0, 0))],
        out_specs=pl.BlockSpec((None, r, N_Q_HEADS * HEAD_DIM), lambda b, i: (b, i, 0)),
        out_shape=jax.ShapeDtypeStruct((bsz, s, N_Q_HEADS * HEAD_DIM), BF16),
        compiler_params=_cparams(("parallel", "arbitrary"), 48),
        name="attn_dense",
    )(qbt, kb, vbt)


def _residual_ffn_norm(x_ref, y, g1_ref, gain_ref, sc_ref, sh_ref, xo_ref, f_ref):
    x_new = x_ref[...] + g1_ref[...] * y
    xo_ref[...] = x_new
    f_ref[...] = _rms_rows(x_new, gain_ref[...]) * (1.0 + sc_ref[...]) + sh_ref[...]


def _attn_out_kernel(x_ref, oa_ref, ob_ref, wa_ref, wb_ref, g1_ref, gain_ref, sc_ref, sh_ref, xo_ref, f_ref):
    y = _dot(oa_ref[...], wa_ref[...]) + _dot(ob_ref[...], wb_ref[...])
    _residual_ffn_norm(x_ref, y, g1_ref, gain_ref, sc_ref, sh_ref, xo_ref, f_ref)


def _attn_out(x_all, oa, ob, wa, wb, g1, gain, sc, sh):
    bsz, s, d = x_all.shape
    r = ROW_BLOCK
    row = lambda w: pl.BlockSpec((None, r, w), lambda b, i: (b, i, 0))
    return pl.pallas_call(
        _attn_out_kernel,
        grid=(bsz, s // r),
        in_specs=[row(d), row(oa.shape[2]), row(ob.shape[2]), _const_spec(wa.shape), _const_spec(wb.shape),
                  _seg_spec(d), _const_spec((1, d)), _seg_spec(d), _seg_spec(d)],
        out_specs=[row(d), row(d)],
        out_shape=[jax.ShapeDtypeStruct((bsz, s, d), F32)] * 2,
        compiler_params=_cparams(("parallel", "arbitrary"), 40),
        name="attn_out",
    )(x_all, oa, ob, wa, wb, g1, gain, sc, sh)


def _gla_out_kernel(x_ref, of_ref, ob_ref, r_ref, hg_ref, w_ref, g1_ref, gain_ref, sc_ref, sh_ref, xo_ref, f_ref):
    o = of_ref[...] + ob_ref[...]
    rr = r_ref[...]
    dv = o.shape[1] // GLA_HEADS
    y = None
    for h in range(GLA_HEADS):
        sl = slice(h * dv, (h + 1) * dv)
        rh = rr[:, sl]
        t = _rms_rows(o[:, sl], hg_ref[...]) * (rh * (1.0 / (1.0 + jnp.exp(-rh))))
        part = _dot(t.astype(BF16), w_ref[sl, :])
        y = part if y is None else y + part
    _residual_ffn_norm(x_ref, y, g1_ref, gain_ref, sc_ref, sh_ref, xo_ref, f_ref)


def _gla_out(x_all, o_f, o_b, r_all, head_gain, w_out, g1, gain, sc, sh, n_ctx):
    bsz, s, d = x_all.shape
    r = ROW_BLOCK
    skip = n_ctx // r
    row_in = lambda w: pl.BlockSpec((None, r, w), lambda b, i: (b, i + skip, 0))
    row_out = pl.BlockSpec((None, r, d), lambda b, i: (b, i, 0))
    return pl.pallas_call(
        _gla_out_kernel,
        grid=(bsz, (s - n_ctx) // r),
        in_specs=[row_in(d), row_in(d), row_in(d), row_in(d), _const_spec(head_gain.shape), _const_spec(w_out.shape),
                  _lat_spec(d), _const_spec((1, d)), _lat_spec(d), _lat_spec(d)],
        out_specs=[row_out, row_out],
        out_shape=[jax.ShapeDtypeStruct((bsz, s - n_ctx, d), F32)] * 2,
        compiler_params=_cparams(("parallel", "arbitrary"), 40),
        name="gla_out",
    )(x_all, o_f, o_b, r_all, head_gain, w_out, g1, gain, sc, sh)


def _peer_route_kernel(f_ref, wq_ref, keys_ref, gates_ref, experts_ref, q_ref, s_top_ref, i_top_ref):
    r = f_ref.shape[0]
    k16 = PEER_TOPK
    q_ref[...] = _dot(f_ref[...].astype(BF16), wq_ref[...]).astype(BF16)
    key_id = lax.broadcasted_iota(I32, (PEER_N_KEYS, r), 0).astype(F32)
    minus_inf = jnp.float32(-jnp.inf)

    def sub_key_topk(hp, carry):
        qs = q_ref[:, pl.ds(pl.multiple_of(hp * PEER_N_KEYS, PEER_N_KEYS), PEER_N_KEYS)]
        scores = _dot_nt(keys_ref[hp], qs)

        def take(k, v):
            m = jnp.max(v, axis=0, keepdims=True)
            idx = jnp.min(jnp.where(v == m, key_id, float(PEER_N_KEYS)), axis=0, keepdims=True)
            s_top_ref[hp, pl.ds(k, 1), :] = m
            i_top_ref[hp, pl.ds(k, 1), :] = idx
            return jnp.where(key_id == idx, minus_inf, v)

        lax.fori_loop(0, k16, take, scores)
        return carry

    lax.fori_loop(0, 2 * PEER_HEADS, sub_key_topk, 0)

    row8 = lax.broadcasted_iota(I32, (SUBLANES, r), 0)
    n_blocks = k16 + 1
    pos = jnp.concatenate([(row8 + (a_blk * SUBLANES if a_blk < 2 else (a_blk - 1) * k16)).astype(F32)
                           for a_blk in range(n_blocks)], axis=0)

    def product_topk(h, carry):
        s1, s2 = s_top_ref[2 * h], s_top_ref[2 * h + 1]
        i1, i2 = i_top_ref[2 * h], i_top_ref[2 * h + 1]
        cand, ids = [], []
        for blk in range(n_blocks):
            a = 0 if blk < 2 else blk - 1
            lo = SUBLANES if blk == 1 else 0
            c = s1[a:a + 1, :] + s2[lo:lo + SUBLANES, :]
            if a > 0:
                c = jnp.where(row8 < k16 // (a + 1), c, minus_inf)
            cand.append(c)
            ids.append(i1[a:a + 1, :] * PEER_N_KEYS + i2[lo:lo + SUBLANES, :])
        cand = jnp.concatenate(cand, axis=0)
        ids = jnp.concatenate(ids, axis=0)
        best = []
        for k in range(k16):
            m = jnp.max(cand, axis=0, keepdims=True)
            sel = jnp.min(jnp.where(cand == m, pos, float(k16 * k16)), axis=0, keepdims=True)
            hit = pos == sel
            expert = jnp.max(jnp.where(hit, ids, -1.0), axis=0, keepdims=True)
            experts_ref[pl.ds(h * k16 + k, 1), :] = expert.astype(I32)
            cand = jnp.where(hit, minus_inf, cand)
            best.append(m)
        e = [jnp.exp(b - best[0]) for b in best]
        den = functools.reduce(lambda x, y: x + y, e)
        for k in range(k16):
            gates_ref[pl.ds(h * k16 + k, 1), :] = e[k] / den
        return carry

    lax.fori_loop(0, PEER_HEADS, product_topk, 0)


def _peer_route(f_flat, wq_bf16, keys_bf16):
    t, d = f_flat.shape
    r = ROW_BLOCK
    nq = wq_bf16.shape[1]
    out = pl.BlockSpec((PEER_PAIRS, r), lambda i: (0, i))
    return pl.pallas_call(
        _peer_route_kernel,
        grid=(t // r,),
        in_specs=[pl.BlockSpec((r, d), lambda i: (i, 0)), pl.BlockSpec((d, nq), lambda i: (0, 0)),
                  pl.BlockSpec(keys_bf16.shape, lambda i: (0, 0, 0))],
        out_specs=[out, out],
        out_shape=[jax.ShapeDtypeStruct((PEER_PAIRS, t), F32), jax.ShapeDtypeStruct((PEER_PAIRS, t), I32)],
        scratch_shapes=[pltpu.VMEM((r, nq), BF16),
                        pltpu.VMEM((2 * PEER_HEADS, PEER_TOPK, r), F32),
                        pltpu.VMEM((2 * PEER_HEADS, PEER_TOPK, r), F32)],
        compiler_params=_cparams(("parallel",), 40),
        name="peer_route",
    )(f_flat, wq_bf16, keys_bf16)


def _pack_table(tbl):
    e, d = tbl.shape
    bits = lax.bitcast_convert_type(tbl.astype(BF16), jnp.uint16).astype(jnp.uint32).reshape(e // 2, 2, d)
    words = (bits[:, 0, :] << 16) | bits[:, 1, :]
    return lax.bitcast_convert_type(words, I32).reshape(e // 2 * (d // LANES), LANES)


def _expert_row(tbl_ref, row8, shift):
    w = tbl_ref[pl.ds(pl.multiple_of(row8, SUBLANES), SUBLANES), :]
    return pltpu.bitcast(lax.shift_left(w, jnp.full(w.shape, shift, I32)) & jnp.int32(-65536), F32)


def _fold8(vs, row):
    vs = [vs[n] for n in (0, 4, 2, 6, 1, 5, 3, 7)]
    step = SUBLANES // 2
    while len(vs) > 1:
        keep = (row % (2 * step)) < step
        nxt = []
        for a, b in zip(vs[0::2], vs[1::2]):
            if 2 * step == SUBLANES:
                nxt.append(jnp.where(keep, a, b) + pltpu.roll(jnp.where(keep, b, a), step, 0))
            else:
                nxt.append(jnp.where(keep, a + pltpu.roll(a, SUBLANES - step, 0), b + pltpu.roll(b, step, 0)))
        vs, step = nxt, step // 2
    return vs[0]


def _index_stream(hbm_refs, smem_refs, sem_ref, i, n_steps, width):
    slot = i % 2

    def copies(step, slot_):
        return [pltpu.make_async_copy(h.at[pl.ds(pl.multiple_of(step * width, width), width)],
                                      s.at[pl.ds(pl.multiple_of(slot_ * width, width), width)],
                                      sem_ref.at[n, slot_])
                for n, (h, s) in enumerate(zip(hbm_refs, smem_refs))]

    @pl.when(i == 0)
    def _():
        for c in copies(0, 0):
            c.start()

    for c in copies(i, slot):
        c.wait()

    @pl.when(i + 1 < n_steps)
    def _():
        for c in copies(i + 1, 1 - slot):
            c.start()

    return slot * width


def _load_table_once(tbl_hbm, tbl_ref, sem_ref, i):
    @pl.when(i == 0)
    def _():
        c = pltpu.make_async_copy(tbl_hbm, tbl_ref, sem_ref.at[0])
        c.start()
        c.wait()


def _erf(x):
    x = jnp.clip(x, -4.0, 4.0)
    z = x * x
    p = -2.72614225801306e-10
    for c in (2.77068142495902e-08, -2.10102402082508e-06, -5.69250639462346e-05, -7.34990630326855e-04,
              -2.95459980854025e-03, -1.60960333262415e-02):
        p = p * z + c
    q = -1.45660718464996e-05
    for c in (-2.13374055278905e-04, -1.68282697438203e-03, -7.37332916720468e-03, -1.42647390514189e-02):
        q = q * z + c
    return x * p / q


def _peer_up_kernel(row_hbm, shift_hbm, tbl_hbm, x_ref, gate_ref, o_ref,
                    row_smem, shift_smem, tbl_ref, part_ref, isem, tsem):
    i = pl.program_id(0)
    tb = x_ref.shape[0]
    width = tb * PEER_PAIRS
    _load_table_once(tbl_hbm, tbl_ref, tsem, i)
    base = _index_stream((row_hbm, shift_hbm), (row_smem, shift_smem), isem, i, pl.num_programs(0), width)
    row = lax.broadcasted_iota(I32, (SUBLANES, LANES), 0)

    def token(t, carry):
        xt = x_ref[t]
        off = base + t * PEER_PAIRS
        for g in range(PEER_PAIRS // SUBLANES):
            prods = []
            for s in range(SUBLANES):
                n = off + g * SUBLANES + s
                prods.append(_expert_row(tbl_ref, row_smem[n], shift_smem[n]) * xt)
            part_ref[pl.ds(pl.multiple_of(t * PEER_PAIRS + g * SUBLANES, SUBLANES), SUBLANES), :] = _fold8(prods, row)
        return carry

    lax.fori_loop(0, tb, token, 0)

    ones = jnp.ones((SUBLANES, LANES), BF16)

    def lane_sums(g, carry):
        a = part_ref[pl.ds(pl.multiple_of(g * SUBLANES * PEER_PAIRS, SUBLANES * PEER_PAIRS), SUBLANES * PEER_PAIRS), :]
        hi = a.astype(BF16)
        lo = (a - hi.astype(F32)).astype(BF16)
        sums = _dot_nt(ones, hi) + _dot_nt(ones, lo)
        for k in range(SUBLANES):
            o_ref[pl.ds(g * SUBLANES + k, 1), :] = sums[0:1, k * PEER_PAIRS:(k + 1) * PEER_PAIRS]
        return carry

    lax.fori_loop(0, tb // SUBLANES, lane_sums, 0)
    act = o_ref[...]
    o_ref[...] = gate_ref[...] * (0.5 * act * (1.0 + _erf(act * (2.0 ** -0.5))))


def _peer_up(rows, shifts, tbl_packed, f_tiles, gates):
    t = f_tiles.shape[0]
    tb = PEER_BLOCK
    width = tb * PEER_PAIRS
    return pl.pallas_call(
        _peer_up_kernel,
        grid=(t // tb,),
        in_specs=[pl.BlockSpec(memory_space=pl.ANY), pl.BlockSpec(memory_space=pl.ANY),
                  pl.BlockSpec(memory_space=pl.ANY),
                  pl.BlockSpec((tb, SUBLANES, LANES), lambda i: (i, 0, 0)),
                  pl.BlockSpec((tb, PEER_PAIRS), lambda i: (i, 0))],
        out_specs=pl.BlockSpec((tb, PEER_PAIRS), lambda i: (i, 0)),
        out_shape=jax.ShapeDtypeStruct((t, PEER_PAIRS), F32),
        scratch_shapes=[pltpu.SMEM((2 * width,), I32), pltpu.SMEM((2 * width,), I32),
                        pltpu.VMEM(tbl_packed.shape, I32),
                        pltpu.VMEM((tb * PEER_PAIRS, LANES), F32),
                        pltpu.SemaphoreType.DMA((2, 2)), pltpu.SemaphoreType.DMA((1,))],
        compiler_params=_cparams(("arbitrary",), 52),
        name="peer_up",
    )(rows, shifts, tbl_packed, f_tiles, gates)


def _peer_down_kernel(row_hbm, coef_hbm, tbl_hbm, x_ref, g2_ref, o_ref,
                      row_smem, coef_smem, tbl_ref, isem, tsem):
    i = pl.program_id(0)
    tb = x_ref.shape[0]
    width = tb * PEER_PAIRS
    _load_table_once(tbl_hbm, tbl_ref, tsem, i)
    base = _index_stream((row_hbm, coef_hbm), (row_smem, coef_smem), isem, i, pl.num_programs(0), width)
    g2 = g2_ref[...]
    n_acc = 4

    high = jnp.int32(-65536)

    def token(t, carry):
        off = base + t * PEER_PAIRS
        accs = [jnp.zeros((SUBLANES, LANES), F32)] * n_acc
        for p in range(PEER_PAIRS):
            n = off + p
            w = tbl_ref[pl.ds(pl.multiple_of(row_smem[n], SUBLANES), SUBLANES), :]
            cw = jnp.full((SUBLANES, LANES), coef_smem[n], I32)
            val = pltpu.bitcast(lax.shift_left(w, cw & 0xFFFF) & high, F32)
            accs[p % n_acc] = accs[p % n_acc] + val * pltpu.bitcast(cw & high, F32)
        o_ref[t] = x_ref[t] + g2 * ((accs[0] + accs[1]) + (accs[2] + accs[3]))
        return carry

    lax.fori_loop(0, tb, token, 0)


def _peer_down(rows, coef_words, tbl_packed, x_tiles, g2_tiles, g2_index):
    t = x_tiles.shape[0]
    tb = PEER_BLOCK
    width = tb * PEER_PAIRS
    tok = pl.BlockSpec((tb, SUBLANES, LANES), lambda i: (i, 0, 0))
    return pl.pallas_call(
        _peer_down_kernel,
        grid=(t // tb,),
        in_specs=[pl.BlockSpec(memory_space=pl.ANY)] * 3
        + [tok, pl.BlockSpec((None, SUBLANES, LANES), lambda i: (g2_index(i), 0, 0))],
        out_specs=tok,
        out_shape=jax.ShapeDtypeStruct(x_tiles.shape, F32),
        scratch_shapes=[pltpu.SMEM((2 * width,), I32), pltpu.SMEM((2 * width,), I32),
                        pltpu.VMEM(tbl_packed.shape, I32),
                        pltpu.SemaphoreType.DMA((2, 2)), pltpu.SemaphoreType.DMA((1,))],
        compiler_params=_cparams(("arbitrary",), 52),
        name="peer_down",
    )(rows, coef_words, tbl_packed, x_tiles, g2_tiles)


def _peer_ffn(f, x_res, g2_tiles, g2_index, wq_bf16, keys_bf16, u_packed, v_packed):
    t, d = f.shape
    gates_t, experts_t = _peer_route(f, wq_bf16, keys_bf16)
    experts = experts_t.T.reshape(-1)
    rows = (experts >> 1) * SUBLANES
    shifts = (experts & 1) * 16
    act = _peer_up(rows, shifts, u_packed, f.reshape(t, SUBLANES, LANES), gates_t.T)
    coef_bits = lax.bitcast_convert_type(act.reshape(-1).astype(BF16), jnp.uint16).astype(I32)
    out = _peer_down(rows, (coef_bits << 16) | shifts, v_packed, x_res.reshape(t, SUBLANES, LANES), g2_tiles, g2_index)
    return out.reshape(t, d)


def _gla_in_kernel(x_ref, gain_ref, sc_ref, sh_ref, w_ref, wg_ref, gw_ref, gb_ref,
                   q_ref, k_ref, v_ref, r_ref, lgf_ref, lgb_ref):
    h = _modulated_norm(x_ref, gain_ref, sc_ref, sh_ref).astype(BF16)
    kd = q_ref.shape[1]
    vd = v_ref.shape[1]
    dk = kd // GLA_HEADS
    q_ref[...] = _dot(h, w_ref[:, 0:kd]) * dk ** -0.5
    k_ref[...] = _dot(h, w_ref[:, kd:2 * kd])
    v_ref[...] = _dot(h, w_ref[:, 2 * kd:2 * kd + vd]).astype(BF16)
    r_ref[...] = _dot(h, w_ref[:, 2 * kd + vd:2 * kd + 2 * vd])
    low_rank = _dot(h, wg_ref[...]).astype(BF16)
    z = _dot(low_rank, gw_ref[...]) + gb_ref[...]
    log_sig = jnp.minimum(z, 0.0) - jnp.log(1.0 + jnp.exp(-jnp.abs(z)))
    lgf_ref[...] = log_sig[:, 0:kd] / GLA_GATE_NORM
    lgb_ref[...] = log_sig[:, kd:2 * kd] / GLA_GATE_NORM


def _gla_in(x_all, gain, sc, sh, w_main, w_gate, gate_w, gate_b, kd, vd):
    bsz, s, d = x_all.shape
    r = ROW_BLOCK
    row = lambda w: pl.BlockSpec((None, r, w), lambda b, i: (b, i, 0))
    shp = lambda w, dt: jax.ShapeDtypeStruct((bsz, s, w), dt)
    return pl.pallas_call(
        _gla_in_kernel,
        grid=(bsz, s // r),
        in_specs=[row(d), _const_spec((1, d)), _seg_spec(d), _seg_spec(d), _const_spec(w_main.shape),
                  _const_spec(w_gate.shape), _const_spec(gate_w.shape), _const_spec(gate_b.shape)],
        out_specs=[row(kd), row(kd), row(vd), row(vd), row(kd), row(kd)],
        out_shape=[shp(kd, F32), shp(kd, F32), shp(vd, BF16), shp(vd, F32), shp(kd, F32), shp(kd, F32)],
        compiler_params=_cparams(("parallel", "arbitrary"), 48),
        name="gla_in",
    )(x_all, gain, sc, sh, w_main, w_gate, gate_w, gate_b)


def _gla_scan_kernel(qf_ref, kf_ref, vf_ref, gf_ref, qb_ref, kb_ref, vb_ref, gb_ref, of_ref, ob_ref,
                     sf_ref, sb_ref):
    i = pl.program_id(1)
    c = GLA_CHUNK
    dk = qf_ref.shape[1] // GLA_HEADS
    dv = vf_ref.shape[1] // GLA_HEADS
    n_chunks = qf_ref.shape[0] // c

    @pl.when(i == 0)
    def _():
        sf_ref[...] = jnp.zeros(sf_ref.shape, F32)
        sb_ref[...] = jnp.zeros(sb_ref.shape, F32)

    t_row = lax.broadcasted_iota(I32, (c, c), 0)
    t_col = lax.broadcasted_iota(I32, (c, c), 1)
    tri = {False: t_col <= t_row, True: t_col >= t_row}
    tri_bf16 = {rev: jnp.where(m, 1.0, 0.0).astype(BF16) for rev, m in tri.items()}

    def chunk(q_ref, k_ref, v_ref, g_ref, o_ref, s_ref, n, rev):
        rows = slice(n * c, (n + 1) * c)
        for h in range(GLA_HEADS):
            kl = slice(h * dk, (h + 1) * dk)
            vl = slice(h * dv, (h + 1) * dv)
            g = g_ref[rows, kl]
            b = _split_dot_left(tri_bf16[rev], g)
            total = jnp.sum(g, axis=0, keepdims=True)
            q_dec = (q_ref[rows, kl] * jnp.exp(b)).astype(BF16)
            kk = k_ref[rows, kl]
            k_end = kk * jnp.exp(total - b)
            k_neg = (kk * jnp.exp(-b)).astype(BF16)
            vv = v_ref[rows, vl]
            state = s_ref[h]
            inter = _dot(q_dec, state.astype(BF16))
            att = jnp.where(tri[rev], _dot_nt(q_dec, k_neg), 0.0)
            o_ref[rows, vl] = inter + _dot(att.astype(BF16), vv)
            decay_col = jnp.exp(jnp.sum(g.T, axis=1, keepdims=True))
            s_ref[h] = decay_col * state + _dot(k_end.T.astype(BF16), vv)

    for n in range(n_chunks):
        chunk(qf_ref, kf_ref, vf_ref, gf_ref, of_ref, sf_ref, n, False)
        chunk(qb_ref, kb_ref, vb_ref, gb_ref, ob_ref, sb_ref, n_chunks - 1 - n, True)


def _split_dot_left(a_bf16, b):
    hi = b.astype(BF16)
    lo = (b - hi.astype(F32)).astype(BF16)
    return _dot(a_bf16, hi) + _dot(a_bf16, lo)


def _gla_scan(q, k, v, lgf, lgb):
    bsz, s, kd = q.shape
    vd = v.shape[2]
    r = ROW_BLOCK
    n = s // r
    fwd = lambda w: pl.BlockSpec((None, r, w), lambda b, i: (b, i, 0))
    bwd = lambda w: pl.BlockSpec((None, r, w), lambda b, i: (b, jnp.where(i == 0, 0, n - i), 0))
    return pl.pallas_call(
        _gla_scan_kernel,
        grid=(bsz, n),
        in_specs=[fwd(kd), fwd(kd), fwd(vd), fwd(kd), bwd(kd), bwd(kd), bwd(vd), bwd(kd)],
        out_specs=[fwd(vd), bwd(vd)],
        out_shape=[jax.ShapeDtypeStruct((bsz, s, vd), F32)] * 2,
        scratch_shapes=[pltpu.VMEM((GLA_HEADS, kd // GLA_HEADS, vd // GLA_HEADS), F32)] * 2,
        compiler_params=_cparams(("parallel", "arbitrary"), 40),
        name="gla_scan",
    )(q, k, v, lgf, q, k, v, lgb)


def _final_norm_kernel(x_ref, gain_ref, o_ref):
    o_ref[...] = _rms_rows(x_ref[...], gain_ref[...])


def _final_norm(x, gain):
    t, d = x.shape
    r = 2 * ROW_BLOCK
    return pl.pallas_call(
        _final_norm_kernel,
        grid=(t // r,),
        in_specs=[pl.BlockSpec((r, d), lambda i: (i, 0)), pl.BlockSpec((1, d), lambda i: (0, 0))],
        out_specs=pl.BlockSpec((r, d), lambda i: (i, 0)),
        out_shape=jax.ShapeDtypeStruct((t, d), F32),
        compiler_params=_cparams(("parallel",), 40),
        name="final_norm",
    )(x, gain)


def _rope_tables(n_ctx, length):
    rows = length // GRID_W
    row = jnp.repeat(jnp.arange(rows), GRID_W)
    col = jnp.tile(jnp.arange(GRID_W), rows)
    pos = jnp.stack([row, col], axis=-1).astype(F32)
    inv = ROPE_THETA ** (-jnp.arange(ROPE_PAIRS, dtype=F32) / ROPE_PAIRS)
    ang = pos[:, :, None] * inv
    cos, sin = jnp.cos(ang), jnp.sin(ang)
    cos64 = jnp.concatenate([cos[:, 0], cos[:, 0], cos[:, 1], cos[:, 1]], axis=-1)
    sin64 = jnp.concatenate([-sin[:, 0], sin[:, 0], -sin[:, 1], sin[:, 1]], axis=-1)
    cos_t = jnp.concatenate([jnp.ones((n_ctx, LANES), F32), jnp.tile(cos64, (1, 2))], axis=0)
    sin_t = jnp.concatenate([jnp.zeros((n_ctx, LANES), F32), jnp.tile(sin64, (1, 2))], axis=0)
    return cos_t, sin_t


def kernel(x, c, ctx, c_ctx, ada_w, ada_b, norm_mix, norm_ffn, attn_w_in, attn_sink, attn_q_gain, attn_k_gain,
           attn_w_out, gla_w_in, gla_gate_w, gla_gate_b, gla_head_gain, gla_w_out, peer_wq, peer_keys, peer_u,
           peer_v, final_norm):
    bsz, length, d = x.shape
    n_ctx = ctx.shape[1]
    s = n_ctx + length
    assert n_ctx == ROW_BLOCK and d == SUBLANES * LANES and length % ROW_BLOCK == 0

    cond8 = jnp.zeros((SUBLANES, d), F32).at[:bsz].set(c).at[bsz].set(c_ctx)
    mod = _adaln(cond8, ada_w, ada_b)

    def seg_vectors(layer):
        lat = mod[layer, :bsz].reshape(bsz, 6, d)
        cx = jnp.broadcast_to(mod[layer, bsz].reshape(1, 6, d), (bsz, 6, d))
        both = jnp.stack([cx, lat], axis=1)
        return [both[:, :, n, :].reshape(bsz, 2, 1, d) for n in range(6)]

    def gate_tiles(g):
        return g.reshape(bsz * 2, SUBLANES, LANES)

    x_all = jnp.concatenate([ctx, x], axis=1)
    row = lambda v: v.reshape(1, -1)

    sh1, sc1, g1, sh2, sc2, g2 = seg_vectors(0)
    cos_t, sin_t = _rope_tables(n_ctx, length)
    lane = jnp.arange(LANES)
    gmean = jnp.where((lane[:, None] // HEAD_DIM) == (lane[None, :] // HEAD_DIM), 1.0 / HEAD_DIM, 0.0).astype(BF16)
    qa, kva, qbt, kb, vbt = _attn_in(x_all, row(norm_mix[0]), sc1, sh1, attn_w_in[0].astype(BF16), cos_t, sin_t,
                                     gmean, row(jnp.tile(attn_q_gain[0], 2)), row(jnp.tile(attn_k_gain[0], 2)))
    oa = _attn_win(attn_sink[0], qa, kva, n_ctx)
    ob = _attn_dense(qbt, kb, vbt, n_ctx)
    w_out = attn_w_out[0].astype(BF16)
    half = w_out.shape[0] // 2
    x_all, f_all = _attn_out(x_all, oa, ob, w_out[:half], w_out[half:], g1, row(norm_ffn[0]), sc2, sh2)
    blocks_per_batch = s // PEER_BLOCK
    ctx_blocks = n_ctx // PEER_BLOCK
    x_all = _peer_ffn(
        f_all.reshape(bsz * s, d), x_all.reshape(bsz * s, d), gate_tiles(g2),
        lambda i: 2 * (i // blocks_per_batch) + jnp.where(i % blocks_per_batch < ctx_blocks, 0, 1),
        peer_wq[0].astype(BF16), peer_keys[0].reshape(2 * PEER_HEADS, PEER_N_KEYS, -1).astype(BF16),
        _pack_table(peer_u[0]), _pack_table(peer_v[0])).reshape(bsz, s, d)

    sh1, sc1, g1, sh2, sc2, g2 = seg_vectors(1)
    w_in = gla_w_in[0]
    kd = gla_gate_w.shape[-1]
    vd = (w_in.shape[1] - 2 * kd - 2 * gla_gate_w.shape[-2]) // 2
    rank = gla_gate_w.shape[-2]
    n_main = 2 * kd + 2 * vd
    w_gate = jnp.zeros((d, LANES), F32).at[:, :2 * rank].set(w_in[:, n_main:]).astype(BF16)
    gate_w = (jnp.zeros((LANES, 2 * kd), F32).at[:rank, :kd].set(gla_gate_w[0, 0])
              .at[rank:2 * rank, kd:].set(gla_gate_w[0, 1])).astype(BF16)
    q, k, v, r_all, lgf, lgb = _gla_in(x_all, row(norm_mix[1]), sc1, sh1, w_in[:, :n_main].astype(BF16), w_gate,
                                       gate_w, gla_gate_b[0].reshape(1, 2 * kd), kd, vd)
    o_f, o_b = _gla_scan(q, k, v, lgf, lgb)
    x_lat, f_lat = _gla_out(x_all, o_f, o_b, r_all, row(gla_head_gain[0]), gla_w_out[0].astype(BF16),
                            g1, row(norm_ffn[1]), sc2, sh2, n_ctx)
    lat_blocks = length // PEER_BLOCK
    x_lat = _peer_ffn(
        f_lat.reshape(bsz * length, d), x_lat.reshape(bsz * length, d), gate_tiles(g2),
        lambda i: 2 * (i // lat_blocks) + 1,
        peer_wq[1].astype(BF16), peer_keys[1].reshape(2 * PEER_HEADS, PEER_N_KEYS, -1).astype(BF16),
        _pack_table(peer_u[1]), _pack_table(peer_v[1]))
    return _final_norm(x_lat, row(final_norm)).reshape(bsz, length, d)
```

```python
import functools
import math

import jax
import jax.numpy as jnp
from jax import lax
from jax.experimental import pallas as pl
from jax.experimental.pallas import tpu as pltpu

F32 = jnp.float32
BF16 = jnp.bfloat16
I32 = jnp.int32

SUBLANES = 8
LANES = 128
VMEM_BYTES_V7X = 64 * 1024 * 1024

EPS = 1e-6
NEG_INF = -1e30
GRID_W = 64
WINDOW = 128
HEAD_DIM = 64
ROPE_THETA = 10000.0
ROPE_PAIRS = HEAD_DIM // 4
N_Q_HEADS = 8
GLA_HEADS = 4
GLA_GATE_NORM = 16.0
GLA_CHUNK = 64
PEER_HEADS = 8
PEER_N_KEYS = 128
PEER_TOPK = 16
PEER_PAIRS = PEER_HEADS * PEER_TOPK
ROW_BLOCK = 256
PEER_BLOCK = 128
KV_CHUNK = 768

NT_DIMS = (((1,), (1,)), ((), ()))


def _cparams(sem, vmem_mb):
    return pltpu.CompilerParams(dimension_semantics=sem, vmem_limit_bytes=vmem_mb * 1024 * 1024)


def _dot(a, b):
    return jnp.dot(a, b, preferred_element_type=F32)


def _dot_nt(a, b):
    return lax.dot_general(a, b, NT_DIMS, preferred_element_type=F32)


def _split_dot(a, b_bf16):
    hi = a.astype(BF16)
    lo = (a - hi.astype(F32)).astype(BF16)
    return _dot(hi, b_bf16) + _dot(lo, b_bf16)


def _rms_rows(x, gain):
    ms = jnp.mean(x * x, axis=-1, keepdims=True)
    return x * lax.rsqrt(ms + EPS) * gain


def _adaln_kernel(cond_ref, w_ref, b_ref, o_ref):
    cnd = cond_ref[...]
    act = cnd * (1.0 / (1.0 + jnp.exp(-cnd)))
    o_ref[...] = jnp.dot(act, w_ref[...], precision=lax.Precision.HIGHEST,
                         preferred_element_type=F32) + b_ref[...]


def _adaln(cond8, ada_w, ada_b):
    depth, d, n6 = ada_w.shape
    tn = n6 // 4
    return pl.pallas_call(
        _adaln_kernel,
        grid=(depth, 4),
        in_specs=[pl.BlockSpec((SUBLANES, d), lambda l, j: (0, 0)),
                  pl.BlockSpec((None, d, tn), lambda l, j: (l, 0, j)),
                  pl.BlockSpec((None, 1, tn), lambda l, j: (l, 0, j))],
        out_specs=pl.BlockSpec((None, SUBLANES, tn), lambda l, j: (l, 0, j)),
        out_shape=jax.ShapeDtypeStruct((depth, SUBLANES, n6), F32),
        compiler_params=_cparams(("arbitrary", "arbitrary"), 40),
        name="adaln",
    )(cond8, ada_w, ada_b.reshape(depth, 1, n6))


def _modulated_norm(x_ref, gain_ref, sc_ref, sh_ref):
    x = x_ref[...]
    return _rms_rows(x, gain_ref[...]) * (1.0 + sc_ref[...]) + sh_ref[...]


def _seg_spec(d):
    return pl.BlockSpec((None, None, 1, d), lambda b, i: (b, jnp.minimum(i, 1), 0, 0))


def _lat_spec(d):
    return pl.BlockSpec((None, None, 1, d), lambda b, i: (b, 1, 0, 0))


def _const_spec(shape):
    nd = len(shape)
    return pl.BlockSpec(shape, lambda b, i: (0,) * nd)


def _swap16(t, lane):
    return jnp.where((lane % 32) < 16, pltpu.roll(t, LANES - 16, 1), pltpu.roll(t, 16, 1))


def _attn_in_kernel(x_ref, gain_ref, sc_ref, sh_ref, w_ref, cos_ref, sin_ref, gmean_ref, qg_ref, kg_ref,
                    qa_ref, kva_ref, qbt_ref, kb_ref, vbt_ref):
    h = _modulated_norm(x_ref, gain_ref, sc_ref, sh_ref).astype(BF16)
    p = _dot(h, w_ref[...])
    lane = lax.broadcasted_iota(I32, (1, LANES), 1)
    low = lane < HEAD_DIM
    cs, sn = cos_ref[...], sin_ref[...]
    scale = HEAD_DIM ** -0.5

    def rope(t):
        return t * cs + _swap16(t, lane) * sn

    def head_norm(t, g_ref):
        ms = _split_dot(t * t, gmean_ref[...])
        return t * lax.rsqrt(ms + EPS) * g_ref[...]

    def tile(j):
        return p[:, j * LANES:(j + 1) * LANES]

    def padded_heads(j, t):
        tr = pltpu.roll(t, HEAD_DIM, 1)
        zero = jnp.zeros_like(t)
        if j // 2 == 0:
            return jnp.where(low, t, zero), jnp.where(low, tr, zero)
        return jnp.where(low, zero, tr), jnp.where(low, zero, t)

    for j in range(4):
        for par, t in enumerate(padded_heads(j, rope(tile(j)) * scale)):
            qa_ref[:, (2 * j + par) * LANES:(2 * j + par + 1) * LANES] = t.astype(BF16)
        for par, t in enumerate(padded_heads(j, rope(head_norm(tile(6 + j), qg_ref)) * scale)):
            qbt_ref[(2 * j + par) * LANES:(2 * j + par + 1) * LANES, :] = t.T.astype(BF16)
    va = tile(5)
    kva_ref[:, 0:LANES] = rope(tile(4)).astype(BF16)
    kva_ref[:, LANES:2 * LANES] = va.astype(BF16)
    kva_ref[:, 2 * LANES:3 * LANES] = pltpu.roll(va, HEAD_DIM, 1).astype(BF16)
    vb = tile(11)
    kb_ref[...] = rope(head_norm(tile(10), kg_ref)).astype(BF16)
    vbt_ref[0:LANES, :] = vb.T.astype(BF16)
    vbt_ref[LANES:2 * LANES, :] = pltpu.roll(vb, HEAD_DIM, 1).T.astype(BF16)


def _attn_in(x_all, gain, sc, sh, w_bf16, cos_t, sin_t, gmean, qg, kg):
    bsz, s, d = x_all.shape
    n = w_bf16.shape[1]
    r = ROW_BLOCK
    row = lambda w: pl.BlockSpec((None, r, w), lambda b, i: (b, i, 0))
    col = lambda w: pl.BlockSpec((None, w, r), lambda b, i: (b, 0, i))
    tab = pl.BlockSpec((r, LANES), lambda b, i: (i, 0))
    return pl.pallas_call(
        _attn_in_kernel,
        grid=(bsz, s // r),
        in_specs=[row(d), _const_spec((1, d)), _seg_spec(d), _seg_spec(d), _const_spec((d, n)), tab, tab,
                  _const_spec((LANES, LANES)), _const_spec((1, LANES)), _const_spec((1, LANES))],
        out_specs=[row(N_Q_HEADS * LANES), row(3 * LANES), col(N_Q_HEADS * LANES), row(LANES), col(2 * LANES)],
        out_shape=[jax.ShapeDtypeStruct((bsz, s, N_Q_HEADS * LANES), BF16),
                   jax.ShapeDtypeStruct((bsz, s, 3 * LANES), BF16),
                   jax.ShapeDtypeStruct((bsz, N_Q_HEADS * LANES, s), BF16),
                   jax.ShapeDtypeStruct((bsz, s, LANES), BF16),
                   jax.ShapeDtypeStruct((bsz, 2 * LANES, s), BF16)],
        compiler_params=_cparams(("parallel", "arbitrary"), 40),
        name="attn_in",
    )(x_all, gain, sc, sh, w_bf16, cos_t, sin_t, gmean, qg, kg)


def _value_tile(kvh, par, v_nat, v_swap):
    return v_nat if kvh == par else v_swap


def _attn_win_kernel(sink_ref, q_ref, cur_ref, prev_ref, next_ref, ctx_ref, o_ref, *, n_ctx, n_tok):
    i = pl.program_id(1)
    r = ROW_BLOCK
    kv = jnp.concatenate([prev_ref[...], cur_ref[...], next_ref[...], ctx_ref[...]], axis=0)
    k, v_nat, v_swap = kv[:, 0:LANES], kv[:, LANES:2 * LANES], kv[:, 2 * LANES:3 * LANES]
    n_band = 2 * r
    qtok = i * r + lax.broadcasted_iota(I32, (r, 1), 0)
    col = lax.broadcasted_iota(I32, (1, n_band + n_ctx), 1)
    ktok = i * r - WINDOW + col
    valid = (col >= n_band) | ((ktok >= n_ctx) & (ktok < n_tok) & (qtok >= n_ctx)
                               & (jnp.abs(qtok - ktok) <= WINDOW))
    low = lax.broadcasted_iota(I32, (1, LANES), 1) < HEAD_DIM
    for j in range(N_Q_HEADS // 2):
        outs = []
        for par in range(2):
            hq = 2 * j + par
            s = _dot_nt(q_ref[:, hq * LANES:(hq + 1) * LANES], k)
            s = jnp.where(valid, s, NEG_INF)
            sink = sink_ref[hq]
            m = jnp.maximum(jnp.max(s, axis=1, keepdims=True), sink)
            e = jnp.exp(s - m)
            den = jnp.sum(e, axis=1, keepdims=True) + jnp.exp(sink - m)
            pv = _dot(e.astype(BF16), _value_tile(j // 2, par, v_nat, v_swap))
            outs.append(pv / den)
        o_ref[:, j * LANES:(j + 1) * LANES] = jnp.where(low, outs[0], outs[1]).astype(BF16)


def _attn_win(sink, qa, kva, n_ctx):
    bsz, s, _ = qa.shape
    r = ROW_BLOCK
    half = r // 2
    last_half = s // half - 1
    kw = 3 * LANES
    return pl.pallas_call(
        functools.partial(_attn_win_kernel, n_ctx=n_ctx, n_tok=s),
        grid=(bsz, s // r),
        in_specs=[pl.BlockSpec(memory_space=pltpu.SMEM),
                  pl.BlockSpec((None, r, N_Q_HEADS * LANES), lambda b, i: (b, i, 0)),
                  pl.BlockSpec((None, r, kw), lambda b, i: (b, i, 0)),
                  pl.BlockSpec((None, half, kw), lambda b, i: (b, jnp.maximum(2 * i - 1, 0), 0)),
                  pl.BlockSpec((None, half, kw), lambda b, i: (b, jnp.minimum(2 * i + 2, last_half), 0)),
                  pl.BlockSpec((None, n_ctx, kw), lambda b, i: (b, 0, 0))],
        out_specs=pl.BlockSpec((None, r, N_Q_HEADS * HEAD_DIM), lambda b, i: (b, i, 0)),
        out_shape=jax.ShapeDtypeStruct((bsz, s, N_Q_HEADS * HEAD_DIM), BF16),
        compiler_params=_cparams(("parallel", "arbitrary"), 40),
        name="attn_win",
    )(sink, qa, kva, kva, kva, kva)


def _attn_dense_kernel(qt_ref, k_ref, vt_ref, o_ref, *, n_ctx, n_tok):
    i = pl.program_id(1)
    r = qt_ref.shape[1]
    low = lax.broadcasted_iota(I32, (LANES, 1), 0) < HEAD_DIM

    def head_pair(j, n_chunks, size):
        def scores(c):
            k = k_ref[pl.ds(pl.multiple_of(c * size, size), size), :]
            return tuple(_dot(k, qt_ref[(2 * j + par) * LANES:(2 * j + par + 1) * LANES, :]) for par in range(2))

        def chunk(c, carry):
            stats, acc, sts = list(carry[:4]), carry[4], carry[5:]
            nxt = scores(jnp.minimum(c + 1, n_chunks - 1))
            start = pl.multiple_of(c * size, size)
            upd = []
            for par in range(2):
                m_old, l_old = stats[2 * par], stats[2 * par + 1]
                m_new = jnp.maximum(m_old, jnp.max(sts[par], axis=0, keepdims=True))
                alpha = jnp.exp(m_old - m_new)
                e = jnp.exp(sts[par] - m_new)
                stats[2 * par] = m_new
                stats[2 * par + 1] = alpha * l_old + jnp.sum(e, axis=0, keepdims=True)
                v0 = 0 if j // 2 == par else LANES
                pv = _dot(vt_ref[v0:v0 + LANES, pl.ds(start, size)], e.astype(BF16))
                upd.append(alpha * acc + pv)
            return (*stats, jnp.where(low, upd[0], upd[1]), *nxt)

        row = lambda v: jnp.full((1, r), v, F32)
        init = (row(NEG_INF), row(0.0), row(NEG_INF), row(0.0), jnp.zeros((LANES, r), F32), *scores(0))
        _, l_e, _, l_o, acc = lax.fori_loop(0, n_chunks, chunk, init)[:5]
        out_t = acc * jnp.where(low, 1.0 / l_e, 1.0 / l_o)
        o_ref[:, j * LANES:(j + 1) * LANES] = out_t.T.astype(BF16)

    @pl.when(i == 0)
    def _():
        for j in range(N_Q_HEADS // 2):
            head_pair(j, 1, n_ctx)

    @pl.when(i > 0)
    def _():
        for j in range(N_Q_HEADS // 2):
            head_pair(j, n_tok // KV_CHUNK, KV_CHUNK)


def _attn_dense(qbt, kb, vbt, n_ctx):
    bsz, s, _ = kb.shape
    r = ROW_BLOCK
    assert s % KV_CHUNK == 0 and n_ctx == r
    return pl.pallas_call(
        functools.partial(_attn_dense_kernel, n_ctx=n_ctx, n_tok=s),
        grid=(bsz, s // r),
        in_specs=[pl.BlockSpec((None, N_Q_HEADS * LANES, r), lambda b, i: (b, 0, i)),
                  pl.BlockSpec((None, s, LANES), lambda b, i: (b, 0, 0)),
                  pl.BlockSpec((None, 2 * LANES, s), lambda b, i: (b, 0, 0))],
        out_specs=pl.BlockSpec((None, r, N_Q_HEADS * HEAD_DIM), lambda b, i: (b, i, 0)),
        out_shape=jax.ShapeDtypeStruct((bsz, s, N_Q_HEADS * HEAD_DIM), BF16),
        compiler_params=_cparams(("parallel", "arbitrary"), 48),
        name="attn_dense",
    )(qbt, kb, vbt)


def _residual_ffn_norm(x_ref, y, g1_ref, gain_ref, sc_ref, sh_ref, xo_ref, f_ref):
    x_new = x_ref[...] + g1_ref[...] * y
    xo_ref[...] = x_new
    f_ref[...] = _rms_rows(x_new, gain_ref[...]) * (1.0 + sc_ref[...]) + sh_ref[...]


def _attn_out_kernel(x_ref, oa_ref, ob_ref, wa_ref, wb_ref, g1_ref, gain_ref, sc_ref, sh_ref, xo_ref, f_ref):
    y = _dot(oa_ref[...], wa_ref[...]) + _dot(ob_ref[...], wb_ref[...])
    _residual_ffn_norm(x_ref, y, g1_ref, gain_ref, sc_ref, sh_ref, xo_ref, f_ref)


def _attn_out(x_all, oa, ob, wa, wb, g1, gain, sc, sh):
    bsz, s, d = x_all.shape
    r = ROW_BLOCK
    row = lambda w: pl.BlockSpec((None, r, w), lambda b, i: (b, i, 0))
    return pl.pallas_call(
        _attn_out_kernel,
        grid=(bsz, s // r),
        in_specs=[row(d), row(oa.shape[2]), row(ob.shape[2]), _const_spec(wa.shape), _const_spec(wb.shape),
                  _seg_spec(d), _const_spec((1, d)), _seg_spec(d), _seg_spec(d)],
        out_specs=[row(d), row(d)],
        out_shape=[jax.ShapeDtypeStruct((bsz, s, d), F32)] * 2,
        compiler_params=_cparams(("parallel", "arbitrary"), 40),
        name="attn_out",
    )(x_all, oa, ob, wa, wb, g1, gain, sc, sh)


def _gla_out_kernel(x_ref, of_ref, ob_ref, r_ref, hg_ref, w_ref, g1_ref, gain_ref, sc_ref, sh_ref, xo_ref, f_ref):
    o = of_ref[...] + ob_ref[...]
    rr = r_ref[...]
    dv = o.shape[1] // GLA_HEADS
    y = None
    for h in range(GLA_HEADS):
        sl = slice(h * dv, (h + 1) * dv)
        rh = rr[:, sl]
        t = _rms_rows(o[:, sl], hg_ref[...]) * (rh * (1.0 / (1.0 + jnp.exp(-rh))))
        part = _dot(t.astype(BF16), w_ref[sl, :])
        y = part if y is None else y + part
    _residual_ffn_norm(x_ref, y, g1_ref, gain_ref, sc_ref, sh_ref, xo_ref, f_ref)


def _gla_out(x_all, o_f, o_b, r_all, head_gain, w_out, g1, gain, sc, sh, n_ctx):
    bsz, s, d = x_all.shape
    r = ROW_BLOCK
    skip = n_ctx // r
    row_in = lambda w: pl.BlockSpec((None, r, w), lambda b, i: (b, i + skip, 0))
    row_out = pl.BlockSpec((None, r, d), lambda b, i: (b, i, 0))
    return pl.pallas_call(
        _gla_out_kernel,
        grid=(bsz, (s - n_ctx) // r),
        in_specs=[row_in(d), row_in(d), row_in(d), row_in(d), _const_spec(head_gain.shape), _const_spec(w_out.shape),
                  _lat_spec(d), _const_spec((1, d)), _lat_spec(d), _lat_spec(d)],
        out_specs=[row_out, row_out],
        out_shape=[jax.ShapeDtypeStruct((bsz, s - n_ctx, d), F32)] * 2,
        compiler_params=_cparams(("parallel", "arbitrary"), 40),
        name="gla_out",
    )(x_all, o_f, o_b, r_all, head_gain, w_out, g1, gain, sc, sh)


def _candidate_blocks(k):
    blocks, seen = [], set()

    def add(kind, fixed, lo):
        new = []
        for off in range(SUBLANES):
            a, b = (fixed, lo + off) if kind == "row" else (lo + off, fixed)
            if (a + 1) * (b + 1) <= k and (a, b) not in seen:
                seen.add((a, b))
                new.append(off)
        if new:
            assert new == list(range(new[0], new[-1] + 1))
            blocks.append((kind, fixed, lo, new[0], new[-1]))

    for a in range(2):
        for lo in range(0, k, SUBLANES):
            add("row", a, lo)
    for b in range(k):
        for lo in range(0, k, SUBLANES):
            add("col", b, lo)
    assert seen == {(a, b) for a in range(k) for b in range(k) if (a + 1) * (b + 1) <= k}
    return blocks


def _peer_route_kernel(f_ref, wq_ref, keys_ref, gates_ref, experts_ref, q_ref, s_top_ref, i_top_ref):
    r = f_ref.shape[0]
    k16 = PEER_TOPK
    q_ref[...] = _dot(f_ref[...].astype(BF16), wq_ref[...]).astype(BF16)
    key_id = lax.broadcasted_iota(I32, (PEER_N_KEYS, r), 0).astype(F32)
    minus_inf = jnp.float32(-jnp.inf)

    def sub_key_topk(hp, carry):
        qs = q_ref[:, pl.ds(pl.multiple_of(hp * PEER_N_KEYS, PEER_N_KEYS), PEER_N_KEYS)]
        scores = _dot_nt(keys_ref[hp], qs)

        def take(k, v):
            m = jnp.max(v, axis=0, keepdims=True)
            idx = jnp.min(jnp.where(v == m, key_id, float(PEER_N_KEYS)), axis=0, keepdims=True)
            s_top_ref[hp, pl.ds(k, 1), :] = m
            i_top_ref[hp, pl.ds(k, 1), :] = idx
            return jnp.where(key_id == idx, minus_inf, v)

        lax.fori_loop(0, k16, take, scores)
        return carry

    lax.fori_loop(0, 2 * PEER_HEADS, sub_key_topk, 0)

    row8 = lax.broadcasted_iota(I32, (SUBLANES, r), 0)
    blocks = _candidate_blocks(k16)
    pos, masks = [], []
    for kind, fixed, lo, first, last in blocks:
        flat = fixed * k16 + lo + row8 if kind == "row" else (lo + row8) * k16 + fixed
        pos.append(flat.astype(F32))
        masks.append(None if (first, last) == (0, SUBLANES - 1) else (row8 >= first) & (row8 <= last))
    pos = jnp.concatenate(pos, axis=0)

    def product_topk(h, carry):
        s1, s2 = s_top_ref[2 * h], s_top_ref[2 * h + 1]
        i1, i2 = i_top_ref[2 * h], i_top_ref[2 * h + 1]
        cand, ids = [], []
        for (kind, fixed, lo, _, _), mask in zip(blocks, masks):
            one, rng = slice(fixed, fixed + 1), slice(lo, lo + SUBLANES)
            sa, sb, ia, ib = (s1[one], s2[rng], i1[one], i2[rng]) if kind == "row" else (s1[rng], s2[one], i1[rng], i2[one])
            c = sa + sb
            cand.append(c if mask is None else jnp.where(mask, c, minus_inf))
            ids.append(ia * PEER_N_KEYS + ib)
        cand = jnp.concatenate(cand, axis=0)
        ids = jnp.concatenate(ids, axis=0)
        best = []
        for k in range(k16):
            m = jnp.max(cand, axis=0, keepdims=True)
            sel = jnp.min(jnp.where(cand == m, pos, float(k16 * k16)), axis=0, keepdims=True)
            hit = pos == sel
            expert = jnp.max(jnp.where(hit, ids, -1.0), axis=0, keepdims=True)
            experts_ref[pl.ds(h * k16 + k, 1), :] = expert.astype(I32)
            cand = jnp.where(hit, minus_inf, cand)
            best.append(m)
        e = [jnp.exp(b - best[0]) for b in best]
        den = functools.reduce(lambda x, y: x + y, e)
        for k in range(k16):
            gates_ref[pl.ds(h * k16 + k, 1), :] = e[k] / den
        return carry

    lax.fori_loop(0, PEER_HEADS, product_topk, 0)


def _peer_route(f_flat, wq_bf16, keys_bf16):
    t, d = f_flat.shape
    r = ROW_BLOCK
    nq = wq_bf16.shape[1]
    out = pl.BlockSpec((PEER_PAIRS, r), lambda i: (0, i))
    return pl.pallas_call(
        _peer_route_kernel,
        grid=(t // r,),
        in_specs=[pl.BlockSpec((r, d), lambda i: (i, 0)), pl.BlockSpec((d, nq), lambda i: (0, 0)),
                  pl.BlockSpec(keys_bf16.shape, lambda i: (0, 0, 0))],
        out_specs=[out, out],
        out_shape=[jax.ShapeDtypeStruct((PEER_PAIRS, t), F32), jax.ShapeDtypeStruct((PEER_PAIRS, t), I32)],
        scratch_shapes=[pltpu.VMEM((r, nq), BF16),
                        pltpu.VMEM((2 * PEER_HEADS, PEER_TOPK, r), F32),
                        pltpu.VMEM((2 * PEER_HEADS, PEER_TOPK, r), F32)],
        compiler_params=_cparams(("parallel",), 40),
        name="peer_route",
    )(f_flat, wq_bf16, keys_bf16)


def _pack_table_kernel(t_ref, o_ref):
    d = o_ref.shape[1]

    def bf16_word(x):
        return pltpu.bitcast(x.astype(BF16).astype(F32), I32)

    o_ref[...] = bf16_word(t_ref[:, 0:d]) | lax.shift_right_logical(bf16_word(t_ref[:, d:2 * d]), 16)


def _pack_table(tbl):
    e, d = tbl.shape
    rows = ROW_BLOCK
    words = pl.pallas_call(
        _pack_table_kernel,
        grid=(e // 2 // rows,),
        in_specs=[pl.BlockSpec((rows, 2 * d), lambda i: (i, 0))],
        out_specs=pl.BlockSpec((rows, d), lambda i: (i, 0)),
        out_shape=jax.ShapeDtypeStruct((e // 2, d), I32),
        compiler_params=_cparams(("parallel",), 40),
        name="pack_table",
    )(tbl.reshape(e // 2, 2 * d))
    return words.reshape(e // 2 * (d // LANES), LANES)


def _expert_row(tbl_ref, row8, shift):
    w = tbl_ref[pl.ds(pl.multiple_of(row8, SUBLANES), SUBLANES), :]
    return pltpu.bitcast(lax.shift_left(w, jnp.full(w.shape, shift, I32)) & jnp.int32(-65536), F32)


def _fold8(vs, row):
    vs = [vs[n] for n in (0, 4, 2, 6, 1, 5, 3, 7)]
    step = SUBLANES // 2
    while len(vs) > 1:
        keep = (row % (2 * step)) < step
        nxt = []
        for a, b in zip(vs[0::2], vs[1::2]):
            if 2 * step == SUBLANES:
                nxt.append(jnp.where(keep, a, b) + pltpu.roll(jnp.where(keep, b, a), step, 0))
            else:
                nxt.append(jnp.where(keep, a + pltpu.roll(a, SUBLANES - step, 0), b + pltpu.roll(b, step, 0)))
        vs, step = nxt, step // 2
    return vs[0]


def _index_stream(hbm_refs, smem_refs, sem_ref, i, n_steps, width):
    slot = i % 2

    def copies(step, slot_):
        return [pltpu.make_async_copy(h.at[pl.ds(pl.multiple_of(step * width, width), width)],
                                      s.at[pl.ds(pl.multiple_of(slot_ * width, width), width)],
                                      sem_ref.at[n, slot_])
                for n, (h, s) in enumerate(zip(hbm_refs, smem_refs))]

    @pl.when(i == 0)
    def _():
        for c in copies(0, 0):
            c.start()

    for c in copies(i, slot):
        c.wait()

    @pl.when(i + 1 < n_steps)
    def _():
        for c in copies(i + 1, 1 - slot):
            c.start()

    return slot * width


def _load_table_once(tbl_hbm, tbl_ref, sem_ref, i):
    @pl.when(i == 0)
    def _():
        c = pltpu.make_async_copy(tbl_hbm, tbl_ref, sem_ref.at[0])
        c.start()
        c.wait()


def _erf(x):
    x = jnp.clip(x, -4.0, 4.0)
    z = x * x
    p = -2.72614225801306e-10
    for c in (2.77068142495902e-08, -2.10102402082508e-06, -5.69250639462346e-05, -7.34990630326855e-04,
              -2.95459980854025e-03, -1.60960333262415e-02):
        p = p * z + c
    q = -1.45660718464996e-05
    for c in (-2.13374055278905e-04, -1.68282697438203e-03, -7.37332916720468e-03, -1.42647390514189e-02):
        q = q * z + c
    return x * p / q


def _peer_up_kernel(row_hbm, shift_hbm, tbl_hbm, x_ref, gate_ref, o_ref,
                    row_smem, shift_smem, tbl_ref, part_ref, isem, tsem):
    i = pl.program_id(0)
    tb = x_ref.shape[0]
    width = tb * PEER_PAIRS
    _load_table_once(tbl_hbm, tbl_ref, tsem, i)
    base = _index_stream((row_hbm, shift_hbm), (row_smem, shift_smem), isem, i, pl.num_programs(0), width)
    row = lax.broadcasted_iota(I32, (SUBLANES, LANES), 0)

    def token(t, carry):
        xt = x_ref[t]
        off = base + t * PEER_PAIRS
        for g in range(PEER_PAIRS // SUBLANES):
            prods = []
            for s in range(SUBLANES):
                n = off + g * SUBLANES + s
                prods.append(_expert_row(tbl_ref, row_smem[n], shift_smem[n]) * xt)
            part_ref[pl.ds(pl.multiple_of(t * PEER_PAIRS + g * SUBLANES, SUBLANES), SUBLANES), :] = _fold8(prods, row)
        return carry

    lax.fori_loop(0, tb, token, 0)

    ones = jnp.ones((SUBLANES, LANES), BF16)

    def lane_sums(g, carry):
        a = part_ref[pl.ds(pl.multiple_of(g * SUBLANES * PEER_PAIRS, SUBLANES * PEER_PAIRS), SUBLANES * PEER_PAIRS), :]
        hi = a.astype(BF16)
        lo = (a - hi.astype(F32)).astype(BF16)
        sums = _dot_nt(ones, hi) + _dot_nt(ones, lo)
        for k in range(SUBLANES):
            o_ref[pl.ds(g * SUBLANES + k, 1), :] = sums[0:1, k * PEER_PAIRS:(k + 1) * PEER_PAIRS]
        return carry

    lax.fori_loop(0, tb // SUBLANES, lane_sums, 0)
    act = o_ref[...]
    o_ref[...] = gate_ref[...] * (0.5 * act * (1.0 + _erf(act * (2.0 ** -0.5))))


def _peer_up(rows, shifts, tbl_packed, f_tiles, gates):
    t = f_tiles.shape[0]
    tb = PEER_BLOCK
    width = tb * PEER_PAIRS
    return pl.pallas_call(
        _peer_up_kernel,
        grid=(t // tb,),
        in_specs=[pl.BlockSpec(memory_space=pl.ANY), pl.BlockSpec(memory_space=pl.ANY),
                  pl.BlockSpec(memory_space=pl.ANY),
                  pl.BlockSpec((tb, SUBLANES, LANES), lambda i: (i, 0, 0)),
                  pl.BlockSpec((tb, PEER_PAIRS), lambda i: (i, 0))],
        out_specs=pl.BlockSpec((tb, PEER_PAIRS), lambda i: (i, 0)),
        out_shape=jax.ShapeDtypeStruct((t, PEER_PAIRS), F32),
        scratch_shapes=[pltpu.SMEM((2 * width,), I32), pltpu.SMEM((2 * width,), I32),
                        pltpu.VMEM(tbl_packed.shape, I32),
                        pltpu.VMEM((tb * PEER_PAIRS, LANES), F32),
                        pltpu.SemaphoreType.DMA((2, 2)), pltpu.SemaphoreType.DMA((1,))],
        compiler_params=_cparams(("arbitrary",), 52),
        name="peer_up",
    )(rows, shifts, tbl_packed, f_tiles, gates)


def _peer_down_kernel(row_hbm, coef_hbm, tbl_hbm, x_ref, g2_ref, o_ref,
                      row_smem, coef_smem, tbl_ref, isem, tsem):
    i = pl.program_id(0)
    tb = x_ref.shape[0]
    width = tb * PEER_PAIRS
    _load_table_once(tbl_hbm, tbl_ref, tsem, i)
    base = _index_stream((row_hbm, coef_hbm), (row_smem, coef_smem), isem, i, pl.num_programs(0), width)
    g2 = g2_ref[...]
    n_acc = 4

    high = jnp.int32(-65536)

    def token(t, carry):
        off = base + t * PEER_PAIRS
        accs = [jnp.zeros((SUBLANES, LANES), F32)] * n_acc
        for p in range(PEER_PAIRS):
            n = off + p
            w = tbl_ref[pl.ds(pl.multiple_of(row_smem[n], SUBLANES), SUBLANES), :]
            cw = jnp.full((SUBLANES, LANES), coef_smem[n], I32)
            val = pltpu.bitcast(lax.shift_left(w, cw & 0xFFFF) & high, F32)
            accs[p % n_acc] = accs[p % n_acc] + val * pltpu.bitcast(cw & high, F32)
        o_ref[t] = x_ref[t] + g2 * ((accs[0] + accs[1]) + (accs[2] + accs[3]))
        return carry

    lax.fori_loop(0, tb, token, 0)


def _peer_down(rows, coef_words, tbl_packed, x_tiles, g2_tiles, g2_index):
    t = x_tiles.shape[0]
    tb = PEER_BLOCK
    width = tb * PEER_PAIRS
    tok = pl.BlockSpec((tb, SUBLANES, LANES), lambda i: (i, 0, 0))
    return pl.pallas_call(
        _peer_down_kernel,
        grid=(t // tb,),
        in_specs=[pl.BlockSpec(memory_space=pl.ANY)] * 3
        + [tok, pl.BlockSpec((None, SUBLANES, LANES), lambda i: (g2_index(i), 0, 0))],
        out_specs=tok,
        out_shape=jax.ShapeDtypeStruct(x_tiles.shape, F32),
        scratch_shapes=[pltpu.SMEM((2 * width,), I32), pltpu.SMEM((2 * width,), I32),
                        pltpu.VMEM(tbl_packed.shape, I32),
                        pltpu.SemaphoreType.DMA((2, 2)), pltpu.SemaphoreType.DMA((1,))],
        compiler_params=_cparams(("arbitrary",), 52),
        name="peer_down",
    )(rows, coef_words, tbl_packed, x_tiles, g2_tiles)


def _peer_ffn(f, x_res, g2_tiles, g2_index, wq_bf16, keys_bf16, u_packed, v_packed):
    t, d = f.shape
    gates_t, experts_t = _peer_route(f, wq_bf16, keys_bf16)
    experts = experts_t.T.reshape(-1)
    rows = (experts >> 1) * SUBLANES
    shifts = (experts & 1) * 16
    act = _peer_up(rows, shifts, u_packed, f.reshape(t, SUBLANES, LANES), gates_t.T)
    coef_bits = lax.bitcast_convert_type(act.reshape(-1).astype(BF16), jnp.uint16).astype(I32)
    out = _peer_down(rows, (coef_bits << 16) | shifts, v_packed, x_res.reshape(t, SUBLANES, LANES), g2_tiles, g2_index)
    return out.reshape(t, d)


def _gla_in_kernel(x_ref, gain_ref, sc_ref, sh_ref, w_ref, wg_ref, gw_ref, gb_ref,
                   q_ref, k_ref, v_ref, r_ref, lgf_ref, lgb_ref):
    h = _modulated_norm(x_ref, gain_ref, sc_ref, sh_ref).astype(BF16)
    kd = q_ref.shape[1]
    vd = v_ref.shape[1]
    dk = kd // GLA_HEADS
    q_ref[...] = _dot(h, w_ref[:, 0:kd]) * dk ** -0.5
    k_ref[...] = _dot(h, w_ref[:, kd:2 * kd])
    v_ref[...] = _dot(h, w_ref[:, 2 * kd:2 * kd + vd]).astype(BF16)
    r_ref[...] = _dot(h, w_ref[:, 2 * kd + vd:2 * kd + 2 * vd])
    low_rank = _dot(h, wg_ref[...]).astype(BF16)
    z = _dot(low_rank, gw_ref[...]) + gb_ref[...]
    log_sig = jnp.minimum(z, 0.0) - jnp.log(1.0 + jnp.exp(-jnp.abs(z)))
    lgf_ref[...] = log_sig[:, 0:kd] / GLA_GATE_NORM
    lgb_ref[...] = log_sig[:, kd:2 * kd] / GLA_GATE_NORM


def _gla_in(x_all, gain, sc, sh, w_main, w_gate, gate_w, gate_b, kd, vd):
    bsz, s, d = x_all.shape
    r = ROW_BLOCK
    row = lambda w: pl.BlockSpec((None, r, w), lambda b, i: (b, i, 0))
    shp = lambda w, dt: jax.ShapeDtypeStruct((bsz, s, w), dt)
    return pl.pallas_call(
        _gla_in_kernel,
        grid=(bsz, s // r),
        in_specs=[row(d), _const_spec((1, d)), _seg_spec(d), _seg_spec(d), _const_spec(w_main.shape),
                  _const_spec(w_gate.shape), _const_spec(gate_w.shape), _const_spec(gate_b.shape)],
        out_specs=[row(kd), row(kd), row(vd), row(vd), row(kd), row(kd)],
        out_shape=[shp(kd, F32), shp(kd, F32), shp(vd, BF16), shp(vd, F32), shp(kd, F32), shp(kd, F32)],
        compiler_params=_cparams(("parallel", "arbitrary"), 48),
        name="gla_in",
    )(x_all, gain, sc, sh, w_main, w_gate, gate_w, gate_b)


def _gla_scan_kernel(qf_ref, kf_ref, vf_ref, gf_ref, qb_ref, kb_ref, vb_ref, gb_ref, of_ref, ob_ref,
                     sf_ref, sb_ref):
    i = pl.program_id(1)
    c = GLA_CHUNK
    dk = qf_ref.shape[1] // GLA_HEADS
    dv = vf_ref.shape[1] // GLA_HEADS
    n_chunks = qf_ref.shape[0] // c

    @pl.when(i == 0)
    def _():
        sf_ref[...] = jnp.zeros(sf_ref.shape, F32)
        sb_ref[...] = jnp.zeros(sb_ref.shape, F32)

    t_row = lax.broadcasted_iota(I32, (c, c), 0)
    t_col = lax.broadcasted_iota(I32, (c, c), 1)
    tri = {False: t_col <= t_row, True: t_col >= t_row}
    tri_bf16 = {rev: jnp.where(m, 1.0, 0.0).astype(BF16) for rev, m in tri.items()}

    def chunk(q_ref, k_ref, v_ref, g_ref, o_ref, s_ref, n, rev):
        rows = slice(n * c, (n + 1) * c)
        for h in range(GLA_HEADS):
            kl = slice(h * dk, (h + 1) * dk)
            vl = slice(h * dv, (h + 1) * dv)
            g = g_ref[rows, kl]
            b = _split_dot_left(tri_bf16[rev], g)
            total = jnp.sum(g, axis=0, keepdims=True)
            q_dec = (q_ref[rows, kl] * jnp.exp(b)).astype(BF16)
            kk = k_ref[rows, kl]
            k_end = kk * jnp.exp(total - b)
            k_neg = (kk * jnp.exp(-b)).astype(BF16)
            vv = v_ref[rows, vl]
            state = s_ref[h]
            inter = _dot(q_dec, state.astype(BF16))
            att = jnp.where(tri[rev], _dot_nt(q_dec, k_neg), 0.0)
            o_ref[rows, vl] = inter + _dot(att.astype(BF16), vv)
            decay_col = jnp.exp(jnp.sum(g.T, axis=1, keepdims=True))
            s_ref[h] = decay_col * state + _dot(k_end.T.astype(BF16), vv)

    for n in range(n_chunks):
        chunk(qf_ref, kf_ref, vf_ref, gf_ref, of_ref, sf_ref, n, False)
        chunk(qb_ref, kb_ref, vb_ref, gb_ref, ob_ref, sb_ref, n_chunks - 1 - n, True)


def _split_dot_left(a_bf16, b):
    hi = b.astype(BF16)
    lo = (b - hi.astype(F32)).astype(BF16)
    return _dot(a_bf16, hi) + _dot(a_bf16, lo)


def _gla_scan(q, k, v, lgf, lgb):
    bsz, s, kd = q.shape
    vd = v.shape[2]
    r = ROW_BLOCK
    n = s // r
    fwd = lambda w: pl.BlockSpec((None, r, w), lambda b, i: (b, i, 0))
    bwd = lambda w: pl.BlockSpec((None, r, w), lambda b, i: (b, jnp.where(i == 0, 0, n - i), 0))
    return pl.pallas_call(
        _gla_scan_kernel,
        grid=(bsz, n),
        in_specs=[fwd(kd), fwd(kd), fwd(vd), fwd(kd), bwd(kd), bwd(kd), bwd(vd), bwd(kd)],
        out_specs=[fwd(vd), bwd(vd)],
        out_shape=[jax.ShapeDtypeStruct((bsz, s, vd), F32)] * 2,
        scratch_shapes=[pltpu.VMEM((GLA_HEADS, kd // GLA_HEADS, vd // GLA_HEADS), F32)] * 2,
        compiler_params=_cparams(("parallel", "arbitrary"), 40),
        name="gla_scan",
    )(q, k, v, lgf, q, k, v, lgb)


def _final_norm_kernel(x_ref, gain_ref, o_ref):
    o_ref[...] = _rms_rows(x_ref[...], gain_ref[...])


def _final_norm(x, gain):
    t, d = x.shape
    r = 2 * ROW_BLOCK
    return pl.pallas_call(
        _final_norm_kernel,
        grid=(t // r,),
        in_specs=[pl.BlockSpec((r, d), lambda i: (i, 0)), pl.BlockSpec((1, d), lambda i: (0, 0))],
        out_specs=pl.BlockSpec((r, d), lambda i: (i, 0)),
        out_shape=jax.ShapeDtypeStruct((t, d), F32),
        compiler_params=_cparams(("parallel",), 40),
        name="final_norm",
    )(x, gain)


def _rope_tables(n_ctx, length):
    rows = length // GRID_W
    row = jnp.repeat(jnp.arange(rows), GRID_W)
    col = jnp.tile(jnp.arange(GRID_W), rows)
    pos = jnp.stack([row, col], axis=-1).astype(F32)
    inv = ROPE_THETA ** (-jnp.arange(ROPE_PAIRS, dtype=F32) / ROPE_PAIRS)
    ang = pos[:, :, None] * inv
    cos, sin = jnp.cos(ang), jnp.sin(ang)
    cos64 = jnp.concatenate([cos[:, 0], cos[:, 0], cos[:, 1], cos[:, 1]], axis=-1)
    sin64 = jnp.concatenate([-sin[:, 0], sin[:, 0], -sin[:, 1], sin[:, 1]], axis=-1)
    cos_t = jnp.concatenate([jnp.ones((n_ctx, LANES), F32), jnp.tile(cos64, (1, 2))], axis=0)
    sin_t = jnp.concatenate([jnp.zeros((n_ctx, LANES), F32), jnp.tile(sin64, (1, 2))], axis=0)
    return cos_t, sin_t


def kernel(x, c, ctx, c_ctx, ada_w, ada_b, norm_mix, norm_ffn, attn_w_in, attn_sink, attn_q_gain, attn_k_gain,
           attn_w_out, gla_w_in, gla_gate_w, gla_gate_b, gla_head_gain, gla_w_out, peer_wq, peer_keys, peer_u,
           peer_v, final_norm):
    bsz, length, d = x.shape
    n_ctx = ctx.shape[1]
    s = n_ctx + length
    assert n_ctx == ROW_BLOCK and d == SUBLANES * LANES and length % ROW_BLOCK == 0

    cond8 = jnp.zeros((SUBLANES, d), F32).at[:bsz].set(c).at[bsz].set(c_ctx)
    mod = _adaln(cond8, ada_w, ada_b)

    def seg_vectors(layer):
        lat = mod[layer, :bsz].reshape(bsz, 6, d)
        cx = jnp.broadcast_to(mod[layer, bsz].reshape(1, 6, d), (bsz, 6, d))
        both = jnp.stack([cx, lat], axis=1)
        return [both[:, :, n, :].reshape(bsz, 2, 1, d) for n in range(6)]

    def gate_tiles(g):
        return g.reshape(bsz * 2, SUBLANES, LANES)

    x_all = jnp.concatenate([ctx, x], axis=1)
    row = lambda v: v.reshape(1, -1)

    sh1, sc1, g1, sh2, sc2, g2 = seg_vectors(0)
    cos_t, sin_t = _rope_tables(n_ctx, length)
    lane = jnp.arange(LANES)
    gmean = jnp.where((lane[:, None] // HEAD_DIM) == (lane[None, :] // HEAD_DIM), 1.0 / HEAD_DIM, 0.0).astype(BF16)
    qa, kva, qbt, kb, vbt = _attn_in(x_all, row(norm_mix[0]), sc1, sh1, attn_w_in[0].astype(BF16), cos_t, sin_t,
                                     gmean, row(jnp.tile(attn_q_gain[0], 2)), row(jnp.tile(attn_k_gain[0], 2)))
    oa = _attn_win(attn_sink[0], qa, kva, n_ctx)
    ob = _attn_dense(qbt, kb, vbt, n_ctx)
    w_out = attn_w_out[0].astype(BF16)
    half = w_out.shape[0] // 2
    x_all, f_all = _attn_out(x_all, oa, ob, w_out[:half], w_out[half:], g1, row(norm_ffn[0]), sc2, sh2)
    blocks_per_batch = s // PEER_BLOCK
    ctx_blocks = n_ctx // PEER_BLOCK
    x_all = _peer_ffn(
        f_all.reshape(bsz * s, d), x_all.reshape(bsz * s, d), gate_tiles(g2),
        lambda i: 2 * (i // blocks_per_batch) + jnp.where(i % blocks_per_batch < ctx_blocks, 0, 1),
        peer_wq[0].astype(BF16), peer_keys[0].reshape(2 * PEER_HEADS, PEER_N_KEYS, -1).astype(BF16),
        _pack_table(peer_u[0]), _pack_table(peer_v[0])).reshape(bsz, s, d)

    sh1, sc1, g1, sh2, sc2, g2 = seg_vectors(1)
    w_in = gla_w_in[0]
    kd = gla_gate_w.shape[-1]
    vd = (w_in.shape[1] - 2 * kd - 2 * gla_gate_w.shape[-2]) // 2
    rank = gla_gate_w.shape[-2]
    n_main = 2 * kd + 2 * vd
    w_gate = jnp.zeros((d, LANES), F32).at[:, :2 * rank].set(w_in[:, n_main:]).astype(BF16)
    gate_w = (jnp.zeros((LANES, 2 * kd), F32).at[:rank, :kd].set(gla_gate_w[0, 0])
              .at[rank:2 * rank, kd:].set(gla_gate_w[0, 1])).astype(BF16)
    q, k, v, r_all, lgf, lgb = _gla_in(x_all, row(norm_mix[1]), sc1, sh1, w_in[:, :n_main].astype(BF16), w_gate,
                                       gate_w, gla_gate_b[0].reshape(1, 2 * kd), kd, vd)
    o_f, o_b = _gla_scan(q, k, v, lgf, lgb)
    x_lat, f_lat = _gla_out(x_all, o_f, o_b, r_all, row(gla_head_gain[0]), gla_w_out[0].astype(BF16),
                            g1, row(norm_ffn[1]), sc2, sh2, n_ctx)
    lat_blocks = length // PEER_BLOCK
    x_lat = _peer_ffn(
        f_lat.reshape(bsz * length, d), x_lat.reshape(bsz * length, d), gate_tiles(g2),
        lambda i: 2 * (i // lat_blocks) + 1,
        peer_wq[1].astype(BF16), peer_keys[1].reshape(2 * PEER_HEADS, PEER_N_KEYS, -1).astype(BF16),
        _pack_table(peer_u[1]), _pack_table(peer_v[1]))
    return _final_norm(x_lat, row(final_norm)).reshape(bsz, length, d)
```

```python
import functools
import math

import jax
import jax.numpy as jnp
from jax import lax
from jax.experimental import pallas as pl
from jax.experimental.pallas import tpu as pltpu

F32 = jnp.float32
BF16 = jnp.bfloat16
I32 = jnp.int32

SUBLANES = 8
LANES = 128
VMEM_BYTES_V7X = 64 * 1024 * 1024

EPS = 1e-6
NEG_INF = -1e30
GRID_W = 64
WINDOW = 128
HEAD_DIM = 64
ROPE_THETA = 10000.0
ROPE_PAIRS = HEAD_DIM // 4
N_Q_HEADS = 8
GLA_HEADS = 4
GLA_GATE_NORM = 16.0
GLA_CHUNK = 64
PEER_HEADS = 8
PEER_N_KEYS = 128
PEER_TOPK = 16
PEER_PAIRS = PEER_HEADS * PEER_TOPK
ROW_BLOCK = 256
PEER_BLOCK = 128
KV_CHUNK = 768

NT_DIMS = (((1,), (1,)), ((), ()))


def _cparams(sem, vmem_mb):
    return pltpu.CompilerParams(dimension_semantics=sem, vmem_limit_bytes=vmem_mb * 1024 * 1024)


def _dot(a, b):
    return jnp.dot(a, b, preferred_element_type=F32)


def _dot_nt(a, b):
    return lax.dot_general(a, b, NT_DIMS, preferred_element_type=F32)


def _split_dot(a, b_bf16):
    hi = a.astype(BF16)
    lo = (a - hi.astype(F32)).astype(BF16)
    return _dot(hi, b_bf16) + _dot(lo, b_bf16)


def _rms_rows(x, gain):
    ms = jnp.mean(x * x, axis=-1, keepdims=True)
    return x * lax.rsqrt(ms + EPS) * gain


def _adaln_kernel(cond_ref, w_ref, b_ref, o_ref):
    cnd = cond_ref[...]
    act = cnd * (1.0 / (1.0 + jnp.exp(-cnd)))
    o_ref[...] = jnp.dot(act, w_ref[...], precision=lax.Precision.HIGHEST,
                         preferred_element_type=F32) + b_ref[...]


def _adaln(cond8, ada_w, ada_b):
    depth, d, n6 = ada_w.shape
    tn = n6 // 4
    return pl.pallas_call(
        _adaln_kernel,
        grid=(depth, 4),
        in_specs=[pl.BlockSpec((SUBLANES, d), lambda l, j: (0, 0)),
                  pl.BlockSpec((None, d, tn), lambda l, j: (l, 0, j)),
                  pl.BlockSpec((None, 1, tn), lambda l, j: (l, 0, j))],
        out_specs=pl.BlockSpec((None, SUBLANES, tn), lambda l, j: (l, 0, j)),
        out_shape=jax.ShapeDtypeStruct((depth, SUBLANES, n6), F32),
        compiler_params=_cparams(("arbitrary", "arbitrary"), 40),
        name="adaln",
    )(cond8, ada_w, ada_b.reshape(depth, 1, n6))


def _modulated_norm(x_ref, gain_ref, sc_ref, sh_ref):
    x = x_ref[...]
    return _rms_rows(x, gain_ref[...]) * (1.0 + sc_ref[...]) + sh_ref[...]


def _seg_spec(d):
    return pl.BlockSpec((None, None, 1, d), lambda b, i: (b, jnp.minimum(i, 1), 0, 0))


def _lat_spec(d):
    return pl.BlockSpec((None, None, 1, d), lambda b, i: (b, 1, 0, 0))


def _const_spec(shape):
    nd = len(shape)
    return pl.BlockSpec(shape, lambda b, i: (0,) * nd)


def _swap16(t, lane):
    return jnp.where((lane % 32) < 16, pltpu.roll(t, LANES - 16, 1), pltpu.roll(t, 16, 1))


def _attn_in_kernel(x_ref, gain_ref, sc_ref, sh_ref, w_ref, cos_ref, sin_ref, gmean_ref, qg_ref, kg_ref,
                    qa_ref, kva_ref, qbt_ref, kb_ref, vbt_ref):
    h = _modulated_norm(x_ref, gain_ref, sc_ref, sh_ref).astype(BF16)
    p = _dot(h, w_ref[...])
    lane = lax.broadcasted_iota(I32, (1, LANES), 1)
    low = lane < HEAD_DIM
    cs, sn = cos_ref[...], sin_ref[...]
    scale = HEAD_DIM ** -0.5

    def rope(t):
        return t * cs + _swap16(t, lane) * sn

    def head_norm(t, g_ref):
        ms = _split_dot(t * t, gmean_ref[...])
        return t * lax.rsqrt(ms + EPS) * g_ref[...]

    def tile(j):
        return p[:, j * LANES:(j + 1) * LANES]

    def padded_heads(j, t):
        tr = pltpu.roll(t, HEAD_DIM, 1)
        zero = jnp.zeros_like(t)
        if j // 2 == 0:
            return jnp.where(low, t, zero), jnp.where(low, tr, zero)
        return jnp.where(low, zero, tr), jnp.where(low, zero, t)

    for j in range(4):
        for par, t in enumerate(padded_heads(j, rope(tile(j)) * scale)):
            qa_ref[:, (2 * j + par) * LANES:(2 * j + par + 1) * LANES] = t.astype(BF16)
        for par, t in enumerate(padded_heads(j, rope(head_norm(tile(6 + j), qg_ref)) * scale)):
            qbt_ref[(2 * j + par) * LANES:(2 * j + par + 1) * LANES, :] = t.T.astype(BF16)
    va = tile(5)
    kva_ref[:, 0:LANES] = rope(tile(4)).astype(BF16)
    kva_ref[:, LANES:2 * LANES] = va.astype(BF16)
    kva_ref[:, 2 * LANES:3 * LANES] = pltpu.roll(va, HEAD_DIM, 1).astype(BF16)
    vb = tile(11)
    kb_ref[...] = rope(head_norm(tile(10), kg_ref)).astype(BF16)
    vbt_ref[0:LANES, :] = vb.T.astype(BF16)
    vbt_ref[LANES:2 * LANES, :] = pltpu.roll(vb, HEAD_DIM, 1).T.astype(BF16)


def _attn_in(x_all, gain, sc, sh, w_bf16, cos_t, sin_t, gmean, qg, kg):
    bsz, s, d = x_all.shape
    n = w_bf16.shape[1]
    r = ROW_BLOCK
    row = lambda w: pl.BlockSpec((None, r, w), lambda b, i: (b, i, 0))
    col = lambda w: pl.BlockSpec((None, w, r), lambda b, i: (b, 0, i))
    tab = pl.BlockSpec((r, LANES), lambda b, i: (i, 0))
    return pl.pallas_call(
        _attn_in_kernel,
        grid=(bsz, s // r),
        in_specs=[row(d), _const_spec((1, d)), _seg_spec(d), _seg_spec(d), _const_spec((d, n)), tab, tab,
                  _const_spec((LANES, LANES)), _const_spec((1, LANES)), _const_spec((1, LANES))],
        out_specs=[row(N_Q_HEADS * LANES), row(3 * LANES), col(N_Q_HEADS * LANES), row(LANES), col(2 * LANES)],
        out_shape=[jax.ShapeDtypeStruct((bsz, s, N_Q_HEADS * LANES), BF16),
                   jax.ShapeDtypeStruct((bsz, s, 3 * LANES), BF16),
                   jax.ShapeDtypeStruct((bsz, N_Q_HEADS * LANES, s), BF16),
                   jax.ShapeDtypeStruct((bsz, s, LANES), BF16),
                   jax.ShapeDtypeStruct((bsz, 2 * LANES, s), BF16)],
        compiler_params=_cparams(("parallel", "arbitrary"), 40),
        name="attn_in",
    )(x_all, gain, sc, sh, w_bf16, cos_t, sin_t, gmean, qg, kg)


def _value_tile(kvh, par, v_nat, v_swap):
    return v_nat if kvh == par else v_swap


def _attn_win_kernel(sink_ref, q_ref, cur_ref, prev_ref, next_ref, ctx_ref, o_ref, *, n_ctx, n_tok):
    i = pl.program_id(1)
    r = ROW_BLOCK
    kv = jnp.concatenate([prev_ref[...], cur_ref[...], next_ref[...], ctx_ref[...]], axis=0)
    k, v_nat, v_swap = kv[:, 0:LANES], kv[:, LANES:2 * LANES], kv[:, 2 * LANES:3 * LANES]
    n_band = 2 * r
    qtok = i * r + lax.broadcasted_iota(I32, (r, 1), 0)
    col = lax.broadcasted_iota(I32, (1, n_band + n_ctx), 1)
    ktok = i * r - WINDOW + col
    valid = (col >= n_band) | ((ktok >= n_ctx) & (ktok < n_tok) & (qtok >= n_ctx)
                               & (jnp.abs(qtok - ktok) <= WINDOW))
    low = lax.broadcasted_iota(I32, (1, LANES), 1) < HEAD_DIM
    for j in range(N_Q_HEADS // 2):
        outs = []
        for par in range(2):
            hq = 2 * j + par
            s = _dot_nt(q_ref[:, hq * LANES:(hq + 1) * LANES], k)
            s = jnp.where(valid, s, NEG_INF)
            sink = sink_ref[hq]
            m = jnp.maximum(jnp.max(s, axis=1, keepdims=True), sink)
            e = jnp.exp(s - m)
            den = jnp.sum(e, axis=1, keepdims=True) + jnp.exp(sink - m)
            pv = _dot(e.astype(BF16), _value_tile(j // 2, par, v_nat, v_swap))
            outs.append(pv / den)
        o_ref[:, j * LANES:(j + 1) * LANES] = jnp.where(low, outs[0], outs[1]).astype(BF16)


def _attn_win(sink, qa, kva, n_ctx):
    bsz, s, _ = qa.shape
    r = ROW_BLOCK
    half = r // 2
    last_half = s // half - 1
    kw = 3 * LANES
    return pl.pallas_call(
        functools.partial(_attn_win_kernel, n_ctx=n_ctx, n_tok=s),
        grid=(bsz, s // r),
        in_specs=[pl.BlockSpec(memory_space=pltpu.SMEM),
                  pl.BlockSpec((None, r, N_Q_HEADS * LANES), lambda b, i: (b, i, 0)),
                  pl.BlockSpec((None, r, kw), lambda b, i: (b, i, 0)),
                  pl.BlockSpec((None, half, kw), lambda b, i: (b, jnp.maximum(2 * i - 1, 0), 0)),
                  pl.BlockSpec((None, half, kw), lambda b, i: (b, jnp.minimum(2 * i + 2, last_half), 0)),
                  pl.BlockSpec((None, n_ctx, kw), lambda b, i: (b, 0, 0))],
        out_specs=pl.BlockSpec((None, r, N_Q_HEADS * HEAD_DIM), lambda b, i: (b, i, 0)),
        out_shape=jax.ShapeDtypeStruct((bsz, s, N_Q_HEADS * HEAD_DIM), BF16),
        compiler_params=_cparams(("parallel", "arbitrary"), 40),
        name="attn_win",
    )(sink, qa, kva, kva, kva, kva)


def _attn_dense_kernel(qt_ref, k_ref, vt_ref, o_ref, *, n_ctx, n_tok):
    i = pl.program_id(1)
    r = qt_ref.shape[1]
    low = lax.broadcasted_iota(I32, (LANES, 1), 0) < HEAD_DIM

    def head_pair(j, n_chunks, size):
        def scores(c):
            k = k_ref[pl.ds(pl.multiple_of(c * size, size), size), :]
            return tuple(_dot(k, qt_ref[(2 * j + par) * LANES:(2 * j + par + 1) * LANES, :]) for par in range(2))

        def chunk(c, carry):
            stats, acc, sts = list(carry[:4]), carry[4], carry[5:]
            nxt = scores(jnp.minimum(c + 1, n_chunks - 1))
            start = pl.multiple_of(c * size, size)
            upd = []
            for par in range(2):
                m_old, l_old = stats[2 * par], stats[2 * par + 1]
                m_new = jnp.maximum(m_old, jnp.max(sts[par], axis=0, keepdims=True))
                alpha = jnp.exp(m_old - m_new)
                e = jnp.exp(sts[par] - m_new)
                stats[2 * par] = m_new
                stats[2 * par + 1] = alpha * l_old + jnp.sum(e, axis=0, keepdims=True)
                v0 = 0 if j // 2 == par else LANES
                pv = _dot(vt_ref[v0:v0 + LANES, pl.ds(start, size)], e.astype(BF16))
                upd.append(alpha * acc + pv)
            return (*stats, jnp.where(low, upd[0], upd[1]), *nxt)

        row = lambda v: jnp.full((1, r), v, F32)
        init = (row(NEG_INF), row(0.0), row(NEG_INF), row(0.0), jnp.zeros((LANES, r), F32), *scores(0))
        _, l_e, _, l_o, acc = lax.fori_loop(0, n_chunks, chunk, init)[:5]
        out_t = acc * jnp.where(low, 1.0 / l_e, 1.0 / l_o)
        o_ref[:, j * LANES:(j + 1) * LANES] = out_t.T.astype(BF16)

    @pl.when(i == 0)
    def _():
        for j in range(N_Q_HEADS // 2):
            head_pair(j, 1, n_ctx)

    @pl.when(i > 0)
    def _():
        for j in range(N_Q_HEADS // 2):
            head_pair(j, n_tok // KV_CHUNK, KV_CHUNK)


def _attn_dense(qbt, kb, vbt, n_ctx):
    bsz, s, _ = kb.shape
    r = ROW_BLOCK
    assert s % KV_CHUNK == 0 and n_ctx == r
    return pl.pallas_call(
        functools.partial(_attn_dense_kernel, n_ctx=n_ctx, n_tok=s),
        grid=(bsz, s // r),
        in_specs=[pl.BlockSpec((None, N_Q_HEADS * LANES, r), lambda b, i: (b, 0, i)),
                  pl.BlockSpec((None, s, LANES), lambda b, i: (b, 0, 0)),
                  pl.BlockSpec((None, 2 * LANES, s), lambda b, i: (b, 0, 0))],
        out_specs=pl.BlockSpec((None, r, N_Q_HEADS * HEAD_DIM), lambda b, i: (b, i, 0)),
        out_shape=jax.ShapeDtypeStruct((bsz, s, N_Q_HEADS * HEAD_DIM), BF16),
        compiler_params=_cparams(("parallel", "arbitrary"), 48),
        name="attn_dense",
    )(qbt, kb, vbt)


def _residual_ffn_norm(x_ref, y, g1_ref, gain_ref, sc_ref, sh_ref, xo_ref, f_ref):
    x_new = x_ref[...] + g1_ref[...] * y
    xo_ref[...] = x_new
    f_ref[...] = _rms_rows(x_new, gain_ref[...]) * (1.0 + sc_ref[...]) + sh_ref[...]


def _attn_out_kernel(x_ref, oa_ref, ob_ref, wa_ref, wb_ref, g1_ref, gain_ref, sc_ref, sh_ref, xo_ref, f_ref):
    y = _dot(oa_ref[...], wa_ref[...]) + _dot(ob_ref[...], wb_ref[...])
    _residual_ffn_norm(x_ref, y, g1_ref, gain_ref, sc_ref, sh_ref, xo_ref, f_ref)


def _attn_out(x_all, oa, ob, wa, wb, g1, gain, sc, sh):
    bsz, s, d = x_all.shape
    r = ROW_BLOCK
    row = lambda w: pl.BlockSpec((None, r, w), lambda b, i: (b, i, 0))
    return pl.pallas_call(
        _attn_out_kernel,
        grid=(bsz, s // r),
        in_specs=[row(d), row(oa.shape[2]), row(ob.shape[2]), _const_spec(wa.shape), _const_spec(wb.shape),
                  _seg_spec(d), _const_spec((1, d)), _seg_spec(d), _seg_spec(d)],
        out_specs=[row(d), row(d)],
        out_shape=[jax.ShapeDtypeStruct((bsz, s, d), F32)] * 2,
        compiler_params=_cparams(("parallel", "arbitrary"), 40),
        name="attn_out",
    )(x_all, oa, ob, wa, wb, g1, gain, sc, sh)


def _gla_out_kernel(x_ref, of_ref, ob_ref, r_ref, hg_ref, w_ref, g1_ref, gain_ref, sc_ref, sh_ref, xo_ref, f_ref):
    o = of_ref[...] + ob_ref[...]
    rr = r_ref[...]
    dv = o.shape[1] // GLA_HEADS
    y = None
    for h in range(GLA_HEADS):
        sl = slice(h * dv, (h + 1) * dv)
        rh = rr[:, sl]
        t = _rms_rows(o[:, sl], hg_ref[...]) * (rh * (1.0 / (1.0 + jnp.exp(-rh))))
        part = _dot(t.astype(BF16), w_ref[sl, :])
        y = part if y is None else y + part
    _residual_ffn_norm(x_ref, y, g1_ref, gain_ref, sc_ref, sh_ref, xo_ref, f_ref)


def _gla_out(x_all, o_f, o_b, r_all, head_gain, w_out, g1, gain, sc, sh, n_ctx):
    bsz, s, d = x_all.shape
    r = ROW_BLOCK
    skip = n_ctx // r
    row_in = lambda w: pl.BlockSpec((None, r, w), lambda b, i: (b, i + skip, 0))
    row_out = pl.BlockSpec((None, r, d), lambda b, i: (b, i, 0))
    return pl.pallas_call(
        _gla_out_kernel,
        grid=(bsz, (s - n_ctx) // r),
        in_specs=[row_in(d), row_in(d), row_in(d), row_in(d), _const_spec(head_gain.shape), _const_spec(w_out.shape),
                  _lat_spec(d), _const_spec((1, d)), _lat_spec(d), _lat_spec(d)],
        out_specs=[row_out, row_out],
        out_shape=[jax.ShapeDtypeStruct((bsz, s - n_ctx, d), F32)] * 2,
        compiler_params=_cparams(("parallel", "arbitrary"), 40),
        name="gla_out",
    )(x_all, o_f, o_b, r_all, head_gain, w_out, g1, gain, sc, sh)


def _candidate_blocks(k):
    blocks, seen = [], set()

    def add(kind, fixed, lo):
        new = []
        for off in range(SUBLANES):
            a, b = (fixed, lo + off) if kind == "row" else (lo + off, fixed)
            if (a + 1) * (b + 1) <= k and (a, b) not in seen:
                seen.add((a, b))
                new.append(off)
        if new:
            assert new == list(range(new[0], new[-1] + 1))
            blocks.append((kind, fixed, lo, new[0], new[-1]))

    for a in range(2):
        for lo in range(0, k, SUBLANES):
            add("row", a, lo)
    for b in range(k):
        for lo in range(0, k, SUBLANES):
            add("col", b, lo)
    assert seen == {(a, b) for a in range(k) for b in range(k) if (a + 1) * (b + 1) <= k}
    return blocks


def _peer_route_kernel(f_ref, wq_ref, keys_ref, gates_ref, experts_ref, q_ref, s_top_ref, i_top_ref):
    r = f_ref.shape[0]
    k16 = PEER_TOPK
    q_ref[...] = _dot(f_ref[...].astype(BF16), wq_ref[...]).astype(BF16)
    key_id = lax.broadcasted_iota(I32, (PEER_N_KEYS, r), 0).astype(F32)
    minus_inf = jnp.float32(-jnp.inf)

    def sub_key_topk(hp, carry):
        qs = q_ref[:, pl.ds(pl.multiple_of(hp * PEER_N_KEYS, PEER_N_KEYS), PEER_N_KEYS)]
        scores = _dot_nt(keys_ref[hp], qs)

        def take(k, v):
            m = jnp.max(v, axis=0, keepdims=True)
            idx = jnp.min(jnp.where(v == m, key_id, float(PEER_N_KEYS)), axis=0, keepdims=True)
            s_top_ref[hp, pl.ds(k, 1), :] = m
            i_top_ref[hp, pl.ds(k, 1), :] = idx
            return jnp.where(key_id == idx, minus_inf, v)

        lax.fori_loop(0, k16, take, scores)
        return carry

    lax.fori_loop(0, 2 * PEER_HEADS, sub_key_topk, 0)

    row8 = lax.broadcasted_iota(I32, (SUBLANES, r), 0)
    blocks = _candidate_blocks(k16)
    pos, masks = [], []
    for kind, fixed, lo, first, last in blocks:
        flat = fixed * k16 + lo + row8 if kind == "row" else (lo + row8) * k16 + fixed
        pos.append(flat.astype(F32))
        masks.append(None if (first, last) == (0, SUBLANES - 1) else (row8 >= first) & (row8 <= last))
    pos = jnp.concatenate(pos, axis=0)

    def product_topk(h, carry):
        s1, s2 = s_top_ref[2 * h], s_top_ref[2 * h + 1]
        i1, i2 = i_top_ref[2 * h], i_top_ref[2 * h + 1]
        cand, ids = [], []
        for (kind, fixed, lo, _, _), mask in zip(blocks, masks):
            one, rng = slice(fixed, fixed + 1), slice(lo, lo + SUBLANES)
            sa, sb, ia, ib = (s1[one], s2[rng], i1[one], i2[rng]) if kind == "row" else (s1[rng], s2[one], i1[rng], i2[one])
            c = sa + sb
            cand.append(c if mask is None else jnp.where(mask, c, minus_inf))
            ids.append(ia * PEER_N_KEYS + ib)
        cand = jnp.concatenate(cand, axis=0)
        ids = jnp.concatenate(ids, axis=0)
        best = []
        for k in range(k16):
            m = jnp.max(cand, axis=0, keepdims=True)
            sel = jnp.min(jnp.where(cand == m, pos, float(k16 * k16)), axis=0, keepdims=True)
            hit = pos == sel
            expert = jnp.max(jnp.where(hit, ids, -1.0), axis=0, keepdims=True)
            experts_ref[pl.ds(h * k16 + k, 1), :] = expert.astype(I32)
            cand = jnp.where(hit, minus_inf, cand)
            best.append(m)
        e = [jnp.exp(b - best[0]) for b in best]
        den = functools.reduce(lambda x, y: x + y, e)
        for k in range(k16):
            gates_ref[pl.ds(h * k16 + k, 1), :] = e[k] / den
        return carry

    lax.fori_loop(0, PEER_HEADS, product_topk, 0)


def _peer_route(f_flat, wq_bf16, keys_bf16):
    t, d = f_flat.shape
    r = ROW_BLOCK
    nq = wq_bf16.shape[1]
    out = pl.BlockSpec((PEER_PAIRS, r), lambda i: (0, i))
    return pl.pallas_call(
        _peer_route_kernel,
        grid=(t // r,),
        in_specs=[pl.BlockSpec((r, d), lambda i: (i, 0)), pl.BlockSpec((d, nq), lambda i: (0, 0)),
                  pl.BlockSpec(keys_bf16.shape, lambda i: (0, 0, 0))],
        out_specs=[out, out],
        out_shape=[jax.ShapeDtypeStruct((PEER_PAIRS, t), F32), jax.ShapeDtypeStruct((PEER_PAIRS, t), I32)],
        scratch_shapes=[pltpu.VMEM((r, nq), BF16),
                        pltpu.VMEM((2 * PEER_HEADS, PEER_TOPK, r), F32),
                        pltpu.VMEM((2 * PEER_HEADS, PEER_TOPK, r), F32)],
        compiler_params=_cparams(("parallel",), 40),
        name="peer_route",
    )(f_flat, wq_bf16, keys_bf16)


def _pack_table_kernel(t_ref, o_ref):
    d = o_ref.shape[1]

    def bf16_word(x):
        return pltpu.bitcast(x.astype(BF16).astype(F32), I32)

    o_ref[...] = bf16_word(t_ref[:, 0:d]) | lax.shift_right_logical(bf16_word(t_ref[:, d:2 * d]), 16)


def _pack_table(tables, layer):
    depth, e, d = tables.shape
    rows = ROW_BLOCK
    n = e // 2 // rows
    words = pl.pallas_call(
        _pack_table_kernel,
        grid=(n,),
        in_specs=[pl.BlockSpec((rows, 2 * d), lambda i: (layer * n + i, 0))],
        out_specs=pl.BlockSpec((rows, d), lambda i: (i, 0)),
        out_shape=jax.ShapeDtypeStruct((e // 2, d), I32),
        compiler_params=_cparams(("parallel",), 40),
        name="pack_table",
    )(tables.reshape(depth * e // 2, 2 * d))
    return words.reshape(e // 2 * (d // LANES), LANES)


def _expert_row(tbl_ref, row8, shift):
    w = tbl_ref[pl.ds(pl.multiple_of(row8, SUBLANES), SUBLANES), :]
    return pltpu.bitcast(lax.shift_left(w, jnp.full(w.shape, shift, I32)) & jnp.int32(-65536), F32)


def _fold8(vs, row):
    vs = [vs[n] for n in (0, 4, 2, 6, 1, 5, 3, 7)]
    step = SUBLANES // 2
    while len(vs) > 1:
        keep = (row % (2 * step)) < step
        nxt = []
        for a, b in zip(vs[0::2], vs[1::2]):
            if 2 * step == SUBLANES:
                nxt.append(jnp.where(keep, a, b) + pltpu.roll(jnp.where(keep, b, a), step, 0))
            else:
                nxt.append(jnp.where(keep, a + pltpu.roll(a, SUBLANES - step, 0), b + pltpu.roll(b, step, 0)))
        vs, step = nxt, step // 2
    return vs[0]


def _index_stream(hbm_refs, smem_refs, sem_ref, i, n_steps, width):
    slot = i % 2

    def copies(step, slot_):
        return [pltpu.make_async_copy(h.at[pl.ds(pl.multiple_of(step * width, width), width)],
                                      s.at[pl.ds(pl.multiple_of(slot_ * width, width), width)],
                                      sem_ref.at[n, slot_])
                for n, (h, s) in enumerate(zip(hbm_refs, smem_refs))]

    @pl.when(i == 0)
    def _():
        for c in copies(0, 0):
            c.start()

    for c in copies(i, slot):
        c.wait()

    @pl.when(i + 1 < n_steps)
    def _():
        for c in copies(i + 1, 1 - slot):
            c.start()

    return slot * width


def _load_table_once(tbl_hbm, tbl_ref, sem_ref, i):
    @pl.when(i == 0)
    def _():
        c = pltpu.make_async_copy(tbl_hbm, tbl_ref, sem_ref.at[0])
        c.start()
        c.wait()


def _erf(x):
    x = jnp.clip(x, -4.0, 4.0)
    z = x * x
    p = -2.72614225801306e-10
    for c in (2.77068142495902e-08, -2.10102402082508e-06, -5.69250639462346e-05, -7.34990630326855e-04,
              -2.95459980854025e-03, -1.60960333262415e-02):
        p = p * z + c
    q = -1.45660718464996e-05
    for c in (-2.13374055278905e-04, -1.68282697438203e-03, -7.37332916720468e-03, -1.42647390514189e-02):
        q = q * z + c
    return x * p / q


def _peer_up_kernel(row_hbm, shift_hbm, tbl_hbm, x_ref, gate_ref, o_ref,
                    row_smem, shift_smem, tbl_ref, part_ref, isem, tsem):
    i = pl.program_id(0)
    tb = x_ref.shape[0]
    width = tb * PEER_PAIRS
    _load_table_once(tbl_hbm, tbl_ref, tsem, i)
    base = _index_stream((row_hbm, shift_hbm), (row_smem, shift_smem), isem, i, pl.num_programs(0), width)
    row = lax.broadcasted_iota(I32, (SUBLANES, LANES), 0)

    def token(t, carry):
        xt = x_ref[t]
        off = base + t * PEER_PAIRS
        for g in range(PEER_PAIRS // SUBLANES):
            prods = []
            for s in range(SUBLANES):
                n = off + g * SUBLANES + s
                prods.append(_expert_row(tbl_ref, row_smem[n], shift_smem[n]) * xt)
            part_ref[pl.ds(pl.multiple_of(t * PEER_PAIRS + g * SUBLANES, SUBLANES), SUBLANES), :] = _fold8(prods, row)
        return carry

    lax.fori_loop(0, tb, token, 0)

    ones = jnp.ones((SUBLANES, LANES), BF16)

    def lane_sums(g, carry):
        a = part_ref[pl.ds(pl.multiple_of(g * SUBLANES * PEER_PAIRS, SUBLANES * PEER_PAIRS), SUBLANES * PEER_PAIRS), :]
        hi = a.astype(BF16)
        lo = (a - hi.astype(F32)).astype(BF16)
        sums = _dot_nt(ones, hi) + _dot_nt(ones, lo)
        for k in range(SUBLANES):
            o_ref[pl.ds(g * SUBLANES + k, 1), :] = sums[0:1, k * PEER_PAIRS:(k + 1) * PEER_PAIRS]
        return carry

    lax.fori_loop(0, tb // SUBLANES, lane_sums, 0)
    act = o_ref[...]
    o_ref[...] = gate_ref[...] * (0.5 * act * (1.0 + _erf(act * (2.0 ** -0.5))))


def _peer_up(rows, shifts, tbl_packed, f_tiles, gates):
    t = f_tiles.shape[0]
    tb = PEER_BLOCK
    width = tb * PEER_PAIRS
    return pl.pallas_call(
        _peer_up_kernel,
        grid=(t // tb,),
        in_specs=[pl.BlockSpec(memory_space=pl.ANY), pl.BlockSpec(memory_space=pl.ANY),
                  pl.BlockSpec(memory_space=pl.ANY),
                  pl.BlockSpec((tb, SUBLANES, LANES), lambda i: (i, 0, 0)),
                  pl.BlockSpec((tb, PEER_PAIRS), lambda i: (i, 0))],
        out_specs=pl.BlockSpec((tb, PEER_PAIRS), lambda i: (i, 0)),
        out_shape=jax.ShapeDtypeStruct((t, PEER_PAIRS), F32),
        scratch_shapes=[pltpu.SMEM((2 * width,), I32), pltpu.SMEM((2 * width,), I32),
                        pltpu.VMEM(tbl_packed.shape, I32),
                        pltpu.VMEM((tb * PEER_PAIRS, LANES), F32),
                        pltpu.SemaphoreType.DMA((2, 2)), pltpu.SemaphoreType.DMA((1,))],
        compiler_params=_cparams(("arbitrary",), 52),
        name="peer_up",
    )(rows, shifts, tbl_packed, f_tiles, gates)


def _peer_down_kernel(row_hbm, coef_hbm, tbl_hbm, x_ref, g2_ref, o_ref,
                      row_smem, coef_smem, tbl_ref, isem, tsem):
    i = pl.program_id(0)
    tb = x_ref.shape[0]
    width = tb * PEER_PAIRS
    _load_table_once(tbl_hbm, tbl_ref, tsem, i)
    base = _index_stream((row_hbm, coef_hbm), (row_smem, coef_smem), isem, i, pl.num_programs(0), width)
    g2 = g2_ref[...]
    n_acc = 4

    high = jnp.int32(-65536)

    def token(t, carry):
        off = base + t * PEER_PAIRS
        accs = [jnp.zeros((SUBLANES, LANES), F32)] * n_acc
        for p in range(PEER_PAIRS):
            n = off + p
            w = tbl_ref[pl.ds(pl.multiple_of(row_smem[n], SUBLANES), SUBLANES), :]
            cw = jnp.full((SUBLANES, LANES), coef_smem[n], I32)
            val = pltpu.bitcast(lax.shift_left(w, cw & 0xFFFF) & high, F32)
            accs[p % n_acc] = accs[p % n_acc] + val * pltpu.bitcast(cw & high, F32)
        o_ref[t] = x_ref[t] + g2 * ((accs[0] + accs[1]) + (accs[2] + accs[3]))
        return carry

    lax.fori_loop(0, tb, token, 0)


def _peer_down(rows, coef_words, tbl_packed, x_tiles, g2_tiles, g2_index):
    t = x_tiles.shape[0]
    tb = PEER_BLOCK
    width = tb * PEER_PAIRS
    tok = pl.BlockSpec((tb, SUBLANES, LANES), lambda i: (i, 0, 0))
    return pl.pallas_call(
        _peer_down_kernel,
        grid=(t // tb,),
        in_specs=[pl.BlockSpec(memory_space=pl.ANY)] * 3
        + [tok, pl.BlockSpec((None, SUBLANES, LANES), lambda i: (g2_index(i), 0, 0))],
        out_specs=tok,
        out_shape=jax.ShapeDtypeStruct(x_tiles.shape, F32),
        scratch_shapes=[pltpu.SMEM((2 * width,), I32), pltpu.SMEM((2 * width,), I32),
                        pltpu.VMEM(tbl_packed.shape, I32),
                        pltpu.SemaphoreType.DMA((2, 2)), pltpu.SemaphoreType.DMA((1,))],
        compiler_params=_cparams(("arbitrary",), 52),
        name="peer_down",
    )(rows, coef_words, tbl_packed, x_tiles, g2_tiles)


def _peer_ffn(f, x_res, g2_tiles, g2_index, wq_bf16, keys_bf16, u_packed, v_packed):
    t, d = f.shape
    gates_t, experts_t = _peer_route(f, wq_bf16, keys_bf16)
    experts = experts_t.T.reshape(-1)
    rows = (experts >> 1) * SUBLANES
    shifts = (experts & 1) * 16
    act = _peer_up(rows, shifts, u_packed, f.reshape(t, SUBLANES, LANES), gates_t.T)
    coef_bits = lax.bitcast_convert_type(act.reshape(-1).astype(BF16), jnp.uint16).astype(I32)
    out = _peer_down(rows, (coef_bits << 16) | shifts, v_packed, x_res.reshape(t, SUBLANES, LANES), g2_tiles, g2_index)
    return out.reshape(t, d)


def _gla_in_kernel(x_ref, gain_ref, sc_ref, sh_ref, w_ref, wg_ref, gw_ref, gb_ref,
                   q_ref, k_ref, v_ref, r_ref, lgf_ref, lgb_ref):
    h = _modulated_norm(x_ref, gain_ref, sc_ref, sh_ref).astype(BF16)
    kd = q_ref.shape[1]
    vd = v_ref.shape[1]
    dk = kd // GLA_HEADS
    q_ref[...] = _dot(h, w_ref[:, 0:kd]) * dk ** -0.5
    k_ref[...] = _dot(h, w_ref[:, kd:2 * kd])
    v_ref[...] = _dot(h, w_ref[:, 2 * kd:2 * kd + vd]).astype(BF16)
    r_ref[...] = _dot(h, w_ref[:, 2 * kd + vd:2 * kd + 2 * vd])
    low_rank = _dot(h, wg_ref[...]).astype(BF16)
    z = _dot(low_rank, gw_ref[...]) + gb_ref[...]
    log_sig = jnp.minimum(z, 0.0) - jnp.log(1.0 + jnp.exp(-jnp.abs(z)))
    lgf_ref[...] = log_sig[:, 0:kd] / GLA_GATE_NORM
    lgb_ref[...] = log_sig[:, kd:2 * kd] / GLA_GATE_NORM


def _gla_in(x_all, gain, sc, sh, w_main, w_gate, gate_w, gate_b, kd, vd):
    bsz, s, d = x_all.shape
    r = ROW_BLOCK
    row = lambda w: pl.BlockSpec((None, r, w), lambda b, i: (b, i, 0))
    shp = lambda w, dt: jax.ShapeDtypeStruct((bsz, s, w), dt)
    return pl.pallas_call(
        _gla_in_kernel,
        grid=(bsz, s // r),
        in_specs=[row(d), _const_spec((1, d)), _seg_spec(d), _seg_spec(d), _const_spec(w_main.shape),
                  _const_spec(w_gate.shape), _const_spec(gate_w.shape), _const_spec(gate_b.shape)],
        out_specs=[row(kd), row(kd), row(vd), row(vd), row(kd), row(kd)],
        out_shape=[shp(kd, F32), shp(kd, F32), shp(vd, BF16), shp(vd, F32), shp(kd, F32), shp(kd, F32)],
        compiler_params=_cparams(("parallel", "arbitrary"), 48),
        name="gla_in",
    )(x_all, gain, sc, sh, w_main, w_gate, gate_w, gate_b)


def _gla_scan_kernel(qf_ref, kf_ref, vf_ref, gf_ref, qb_ref, kb_ref, vb_ref, gb_ref, of_ref, ob_ref,
                     sf_ref, sb_ref):
    i = pl.program_id(1)
    c = GLA_CHUNK
    dk = qf_ref.shape[1] // GLA_HEADS
    dv = vf_ref.shape[1] // GLA_HEADS
    n_chunks = qf_ref.shape[0] // c

    @pl.when(i == 0)
    def _():
        sf_ref[...] = jnp.zeros(sf_ref.shape, F32)
        sb_ref[...] = jnp.zeros(sb_ref.shape, F32)

    t_row = lax.broadcasted_iota(I32, (c, c), 0)
    t_col = lax.broadcasted_iota(I32, (c, c), 1)
    tri = {False: t_col <= t_row, True: t_col >= t_row}
    tri_bf16 = {rev: jnp.where(m, 1.0, 0.0).astype(BF16) for rev, m in tri.items()}

    def chunk(q_ref, k_ref, v_ref, g_ref, o_ref, s_ref, n, rev):
        rows = slice(n * c, (n + 1) * c)
        for h in range(GLA_HEADS):
            kl = slice(h * dk, (h + 1) * dk)
            vl = slice(h * dv, (h + 1) * dv)
            g = g_ref[rows, kl]
            b = _split_dot_left(tri_bf16[rev], g)
            total = jnp.sum(g, axis=0, keepdims=True)
            q_dec = (q_ref[rows, kl] * jnp.exp(b)).astype(BF16)
            kk = k_ref[rows, kl]
            k_end = kk * jnp.exp(total - b)
            k_neg = (kk * jnp.exp(-b)).astype(BF16)
            vv = v_ref[rows, vl]
            state = s_ref[h]
            inter = _dot(q_dec, state.astype(BF16))
            att = jnp.where(tri[rev], _dot_nt(q_dec, k_neg), 0.0)
            o_ref[rows, vl] = inter + _dot(att.astype(BF16), vv)
            decay_col = jnp.exp(jnp.sum(g.T, axis=1, keepdims=True))
            s_ref[h] = decay_col * state + _dot(k_end.T.astype(BF16), vv)

    for n in range(n_chunks):
        chunk(qf_ref, kf_ref, vf_ref, gf_ref, of_ref, sf_ref, n, False)
        chunk(qb_ref, kb_ref, vb_ref, gb_ref, ob_ref, sb_ref, n_chunks - 1 - n, True)


def _split_dot_left(a_bf16, b):
    hi = b.astype(BF16)
    lo = (b - hi.astype(F32)).astype(BF16)
    return _dot(a_bf16, hi) + _dot(a_bf16, lo)


def _gla_scan(q, k, v, lgf, lgb):
    bsz, s, kd = q.shape
    vd = v.shape[2]
    r = ROW_BLOCK
    n = s // r
    fwd = lambda w: pl.BlockSpec((None, r, w), lambda b, i: (b, i, 0))
    bwd = lambda w: pl.BlockSpec((None, r, w), lambda b, i: (b, jnp.where(i == 0, 0, n - i), 0))
    return pl.pallas_call(
        _gla_scan_kernel,
        grid=(bsz, n),
        in_specs=[fwd(kd), fwd(kd), fwd(vd), fwd(kd), bwd(kd), bwd(kd), bwd(vd), bwd(kd)],
        out_specs=[fwd(vd), bwd(vd)],
        out_shape=[jax.ShapeDtypeStruct((bsz, s, vd), F32)] * 2,
        scratch_shapes=[pltpu.VMEM((GLA_HEADS, kd // GLA_HEADS, vd // GLA_HEADS), F32)] * 2,
        compiler_params=_cparams(("parallel", "arbitrary"), 40),
        name="gla_scan",
    )(q, k, v, lgf, q, k, v, lgb)


def _final_norm_kernel(x_ref, gain_ref, o_ref):
    o_ref[...] = _rms_rows(x_ref[...], gain_ref[...])


def _final_norm(x, gain):
    t, d = x.shape
    r = 2 * ROW_BLOCK
    assert t % r == 0
    return pl.pallas_call(
        _final_norm_kernel,
        grid=(t // r,),
        in_specs=[pl.BlockSpec((r, d), lambda i: (i, 0)), pl.BlockSpec((1, d), lambda i: (0, 0))],
        out_specs=pl.BlockSpec((r, d), lambda i: (i, 0)),
        out_shape=jax.ShapeDtypeStruct((t, d), F32),
        compiler_params=_cparams(("parallel",), 40),
        name="final_norm",
    )(x, gain)


def _rope_tables(n_ctx, length):
    rows = length // GRID_W
    row = jnp.repeat(jnp.arange(rows), GRID_W)
    col = jnp.tile(jnp.arange(GRID_W), rows)
    pos = jnp.stack([row, col], axis=-1).astype(F32)
    inv = ROPE_THETA ** (-jnp.arange(ROPE_PAIRS, dtype=F32) / ROPE_PAIRS)
    ang = pos[:, :, None] * inv
    cos, sin = jnp.cos(ang), jnp.sin(ang)
    cos64 = jnp.concatenate([cos[:, 0], cos[:, 0], cos[:, 1], cos[:, 1]], axis=-1)
    sin64 = jnp.concatenate([-sin[:, 0], sin[:, 0], -sin[:, 1], sin[:, 1]], axis=-1)
    cos_t = jnp.concatenate([jnp.ones((n_ctx, LANES), F32), jnp.tile(cos64, (1, 2))], axis=0)
    sin_t = jnp.concatenate([jnp.zeros((n_ctx, LANES), F32), jnp.tile(sin64, (1, 2))], axis=0)
    return cos_t, sin_t


def kernel(x, c, ctx, c_ctx, ada_w, ada_b, norm_mix, norm_ffn, attn_w_in, attn_sink, attn_q_gain, attn_k_gain,
           attn_w_out, gla_w_in, gla_gate_w, gla_gate_b, gla_head_gain, gla_w_out, peer_wq, peer_keys, peer_u,
           peer_v, final_norm):
    bsz, length, d = x.shape
    n_ctx = ctx.shape[1]
    s = n_ctx + length
    assert n_ctx == ROW_BLOCK and d == SUBLANES * LANES and length % ROW_BLOCK == 0

    cond8 = jnp.zeros((SUBLANES, d), F32).at[:bsz].set(c).at[bsz].set(c_ctx)
    mod = _adaln(cond8, ada_w, ada_b)

    def seg_vectors(layer):
        lat = mod[layer, :bsz].reshape(bsz, 6, d)
        cx = jnp.broadcast_to(mod[layer, bsz].reshape(1, 6, d), (bsz, 6, d))
        both = jnp.stack([cx, lat], axis=1)
        return [both[:, :, n, :].reshape(bsz, 2, 1, d) for n in range(6)]

    def gate_tiles(g):
        return g.reshape(bsz * 2, SUBLANES, LANES)

    x_all = jnp.concatenate([ctx, x], axis=1)
    row = lambda v: v.reshape(1, -1)

    sh1, sc1, g1, sh2, sc2, g2 = seg_vectors(0)
    cos_t, sin_t = _rope_tables(n_ctx, length)
    lane = jnp.arange(LANES)
    gmean = jnp.where((lane[:, None] // HEAD_DIM) == (lane[None, :] // HEAD_DIM), 1.0 / HEAD_DIM, 0.0).astype(BF16)
    qa, kva, qbt, kb, vbt = _attn_in(x_all, row(norm_mix[0]), sc1, sh1, attn_w_in[0].astype(BF16), cos_t, sin_t,
                                     gmean, row(jnp.tile(attn_q_gain[0], 2)), row(jnp.tile(attn_k_gain[0], 2)))
    oa = _attn_win(attn_sink[0], qa, kva, n_ctx)
    ob = _attn_dense(qbt, kb, vbt, n_ctx)
    w_out = attn_w_out[0].astype(BF16)
    half = w_out.shape[0] // 2
    x_all, f_all = _attn_out(x_all, oa, ob, w_out[:half], w_out[half:], g1, row(norm_ffn[0]), sc2, sh2)
    blocks_per_batch = s // PEER_BLOCK
    ctx_blocks = n_ctx // PEER_BLOCK
    x_all = _peer_ffn(
        f_all.reshape(bsz * s, d), x_all.reshape(bsz * s, d), gate_tiles(g2),
        lambda i: 2 * (i // blocks_per_batch) + jnp.where(i % blocks_per_batch < ctx_blocks, 0, 1),
        peer_wq[0].astype(BF16), peer_keys[0].reshape(2 * PEER_HEADS, PEER_N_KEYS, -1).astype(BF16),
        _pack_table(peer_u, 0), _pack_table(peer_v, 0)).reshape(bsz, s, d)

    sh1, sc1, g1, sh2, sc2, g2 = seg_vectors(1)
    w_in = gla_w_in[0]
    kd = gla_gate_w.shape[-1]
    vd = (w_in.shape[1] - 2 * kd - 2 * gla_gate_w.shape[-2]) // 2
    rank = gla_gate_w.shape[-2]
    n_main = 2 * kd + 2 * vd
    w_gate = jnp.zeros((d, LANES), F32).at[:, :2 * rank].set(w_in[:, n_main:]).astype(BF16)
    gate_w = (jnp.zeros((LANES, 2 * kd), F32).at[:rank, :kd].set(gla_gate_w[0, 0])
              .at[rank:2 * rank, kd:].set(gla_gate_w[0, 1])).astype(BF16)
    q, k, v, r_all, lgf, lgb = _gla_in(x_all, row(norm_mix[1]), sc1, sh1, w_in[:, :n_main].astype(BF16), w_gate,
                                       gate_w, gla_gate_b[0].reshape(1, 2 * kd), kd, vd)
    o_f, o_b = _gla_scan(q, k, v, lgf, lgb)
    x_lat, f_lat = _gla_out(x_all, o_f, o_b, r_all, row(gla_head_gain[0]), gla_w_out[0].astype(BF16),
                            g1, row(norm_ffn[1]), sc2, sh2, n_ctx)
    lat_blocks = length // PEER_BLOCK
    x_lat = _peer_ffn(
        f_lat.reshape(bsz * length, d), x_lat.reshape(bsz * length, d), gate_tiles(g2),
        lambda i: 2 * (i // lat_blocks) + 1,
        peer_wq[1].astype(BF16), peer_keys[1].reshape(2 * PEER_HEADS, PEER_N_KEYS, -1).astype(BF16),
        _pack_table(peer_u, 1), _pack_table(peer_v, 1))
    return _final_norm(x_lat, row(final_norm)).reshape(bsz, length, d)
```
